```python
import math
import jax
import jax.numpy as jnp
from jax import lax
import numpy as np

D_MODEL = 4096
BATCH = 4
SEQ = 2048
DEPTH = 1
DEC_BATCH = 32
DEC_SEQ = 4
PAST_LEN = 8192
PAGE_SIZE = 128

HEAD_DIM = 128
MIX_DIM = D_MODEL
N_HEADS = MIX_DIM // HEAD_DIM
H_DN = N_HEADS // 2
H_FOX = N_HEADS - H_DN
DN_DK = HEAD_DIM
DN_DV = HEAD_DIM
DN_QK_DIM = H_DN * DN_DK
DN_V_DIM = H_DN * DN_DV
DN_CONV_DIM = 2 * DN_QK_DIM + DN_V_DIM
CONV_W = 4
DN_CHUNK = 64
FOX_DIM = H_FOX * HEAD_DIM
FOX_Q_BLOCK = 128
IN_SIZES = (DN_CONV_DIM, DN_V_DIM, H_DN, H_DN, FOX_DIM, FOX_DIM, FOX_DIM, H_FOX)
IN_DIM = DN_CONV_DIM + DN_V_DIM + 2 * H_DN + 3 * FOX_DIM + H_FOX
N_EXP = 32
TOP_K = 4
D_FF = D_MODEL
SWIGLU_LIMIT = 7.0
SWIGLU_ALPHA = 1.702
MOE_BLOCK = 128
RMS_EPS = 1e-6
L2_EPS = 1e-6

kernel_name = 'hymba_gdn_fox_moe_step'


def rmsnorm(x, w):
    xf = x.astype(jnp.float32)
    y = xf * lax.rsqrt(jnp.mean(xf * xf, axis=-1, keepdims=True) + RMS_EPS)
    return (y * w.astype(jnp.float32)).astype(x.dtype)


def l2norm(x):
    xf = x.astype(jnp.float32)
    return xf * lax.rsqrt(jnp.sum(xf * xf, axis=-1, keepdims=True) + L2_EPS)


def mix_inputs(h, conv_buf, w_in, conv_w, a_log, dt_bias, fox_f_bias):
    B, L, _ = h.shape
    proj = h @ w_in
    points = [int(p) for p in np.cumsum(IN_SIZES)[:-1]]
    dn_x, dn_z, dn_b, dn_a, f_q, f_k, f_v, f_f = jnp.split(proj, points, axis=-1)
    xpad = jnp.concatenate([conv_buf.astype(proj.dtype), dn_x], axis=1)
    conv = sum(xpad[:, j:j + L] * conv_w[j] for j in range(CONV_W))
    new_buf = xpad[:, L:]
    conv = jax.nn.silu(conv)
    c_q, c_k, c_v = jnp.split(conv, [DN_QK_DIM, 2 * DN_QK_DIM], axis=-1)
    dq = l2norm(c_q.reshape(B, L, H_DN, DN_DK))
    dk = l2norm(c_k.reshape(B, L, H_DN, DN_DK))
    dv = c_v.reshape(B, L, H_DN, DN_DV)
    beta = jax.nn.sigmoid(dn_b.astype(jnp.float32))
    g = -jnp.exp(a_log.astype(jnp.float32)) * jax.nn.softplus(dn_a.astype(jnp.float32) + dt_bias.astype(jnp.float32))
    z = dn_z.reshape(B, L, H_DN, DN_DV)
    fq = f_q.reshape(B, L, H_FOX, HEAD_DIM)
    fk = f_k.reshape(B, L, H_FOX, HEAD_DIM)
    fv = f_v.reshape(B, L, H_FOX, HEAD_DIM)
    logf = jax.nn.log_sigmoid(f_f.astype(jnp.float32) + fox_f_bias.astype(jnp.float32))
    return (dq, dk, dv, beta, g, z, new_buf), (fq, fk, fv, logf)


def gated_delta_rule(q, k, v, g, beta, s0):
    B, L, H, DK = q.shape
    DV = v.shape[-1]
    C = math.gcd(L, DN_CHUNK)
    N = L // C
    f32 = jnp.float32

    def to_chunks(t):
        t = t.astype(f32).reshape((B, N, C) + t.shape[2:])
        return jnp.moveaxis(t, 1, 0)

    qc = to_chunks(q) * (DK ** -0.5)
    kc = to_chunks(k)
    vc = to_chunks(v)
    bc = to_chunks(beta)
    gc = jnp.cumsum(to_chunks(g), axis=2)
    incl = jnp.tril(jnp.ones((C, C), bool))
    strict = jnp.tril(jnp.ones((C, C), bool), -1)
    gh = jnp.moveaxis(gc, 3, 2)
    diff = gh[..., :, None] - gh[..., None, :]
    decay = jnp.where(incl, jnp.exp(jnp.where(incl, diff, 0.0)), 0.0)
    kb = kc * bc[..., None]
    a = jnp.where(strict, jnp.einsum('nbihk,nbjhk->nbhij', kb, kc) * decay, 0.0)
    eye = jnp.eye(C, dtype=f32)
    tinv = lax.linalg.triangular_solve(a + eye, jnp.broadcast_to(eye, a.shape),
                                       left_side=True, lower=True, unit_diagonal=True)
    u = jnp.einsum('nbhij,nbjhv->nbihv', tinv, vc * bc[..., None])
    w = jnp.einsum('nbhij,nbjhk->nbihk', tinv, kb * jnp.exp(gc)[..., None])
    qk = jnp.einsum('nbihk,nbjhk->nbhij', qc, kc) * decay

    def chunk_step(s, xs):
        q_n, k_n, u_n, w_n, g_n, qk_n = xs
        v_new = u_n - jnp.einsum('bihk,bhkv->bihv', w_n, s)
        o = (jnp.einsum('bihk,bhkv->bihv', q_n * jnp.exp(g_n)[..., None], s)
             + jnp.einsum('bhij,bjhv->bihv', qk_n, v_new))
        g_last = g_n[:, -1]
        k_dec = k_n * jnp.exp(g_last[:, None] - g_n)[..., None]
        s = s * jnp.exp(g_last)[..., None, None] + jnp.einsum('bjhk,bjhv->bhkv', k_dec, v_new)
        return s, o

    s_fin, o = lax.scan(chunk_step, s0.astype(f32), (qc, kc, u, w, gc, qk))
    o = jnp.moveaxis(o, 0, 1).reshape(B, L, H, DV)
    return o.astype(v.dtype), s_fin.astype(s0.dtype)


def fox_prompt_attention(q, k, v, logf):
    B, L, H, d = q.shape
    nb = L // FOX_Q_BLOCK
    c = jnp.cumsum(logf.astype(jnp.float32), axis=1)
    ck = jnp.moveaxis(c, 2, 1)
    kpos = jnp.arange(L)

    def block(xs):
        i, qb, cqb = xs
        s = jnp.einsum('bqhd,bkhd->bhqk', qb, k, preferred_element_type=jnp.float32) * (d ** -0.5)
        s = s + jnp.moveaxis(cqb, 2, 1)[..., None] - ck[:, :, None, :]
        qpos = i * FOX_Q_BLOCK + jnp.arange(FOX_Q_BLOCK)
        s = jnp.where(qpos[:, None] >= kpos[None, :], s, -jnp.inf)
        p = jax.nn.softmax(s, axis=-1)
        return jnp.einsum('bhqk,bkhd->bqhd', p.astype(v.dtype), v)

    qb = jnp.moveaxis(q.reshape(B, nb, FOX_Q_BLOCK, H, d), 1, 0)
    cqb = jnp.moveaxis(c.reshape(B, nb, FOX_Q_BLOCK, H), 1, 0)
    o = lax.map(block, (jnp.arange(nb), qb, cqb))
    return jnp.moveaxis(o, 0, 1).reshape(B, L, H, d)


def fox_sample_attention(q, k, v, logf, layer, cache_k, cache_v, cache_logf, page_table):
    B, T, H, d = q.shape
    n_pages = page_table.shape[1]
    ps = cache_k.shape[2]
    f32 = jnp.float32
    lf_past = cache_logf[layer, page_table].astype(f32)
    cum = jnp.cumsum(lf_past.reshape(B, n_pages * ps, H), axis=1)
    c_past = (cum - cum[:, -1:]).reshape(B, n_pages, ps, H)
    c_new = jnp.cumsum(logf.astype(f32), axis=1)
    cq = jnp.moveaxis(c_new, 2, 1)[..., None]
    qf = q.astype(f32) * (d ** -0.5)

    def page_step(carry, xs):
        m, l_sum, acc = carry
        phys, c_pg = xs
        kp = cache_k[layer, phys].astype(f32)
        vp = cache_v[layer, phys].astype(f32)
        s = jnp.einsum('bthd,bshd->bhts', qf, kp) + cq - jnp.moveaxis(c_pg, 2, 1)[:, :, None, :]
        m_new = jnp.maximum(m, jnp.max(s, axis=-1))
        p = jnp.exp(s - m_new[..., None])
        corr = jnp.exp(m - m_new)
        return (m_new, l_sum * corr + jnp.sum(p, axis=-1),
                acc * corr[..., None] + jnp.einsum('bhts,bshd->bhtd', p, vp)), None

    init = (jnp.full((B, H, T), -jnp.inf, f32), jnp.zeros((B, H, T), f32), jnp.zeros((B, H, T, d), f32))
    (m, l_sum, acc), _ = lax.scan(page_step, init, (page_table.T, jnp.moveaxis(c_past, 1, 0)))
    s = jnp.einsum('bthd,bshd->bhts', qf, k.astype(f32)) + cq - jnp.moveaxis(c_new, 2, 1)[:, :, None, :]
    s = jnp.where(jnp.tril(jnp.ones((T, T), bool)), s, -jnp.inf)
    m_new = jnp.maximum(m, jnp.max(s, axis=-1))
    p = jnp.exp(s - m_new[..., None])
    corr = jnp.exp(m - m_new)
    l_sum = l_sum * corr + jnp.sum(p, axis=-1)
    acc = acc * corr[..., None] + jnp.einsum('bhts,bshd->bhtd', p, v.astype(f32))
    o = acc / l_sum[..., None]
    return jnp.moveaxis(o, 1, 2).astype(v.dtype)


def mix_output(o_dn, z, o_fox, dn_norm_w, fox_norm_w, w_out):
    B, L = o_dn.shape[:2]
    dn = rmsnorm(o_dn, dn_norm_w) * jax.nn.silu(z)
    fx = rmsnorm(o_fox, fox_norm_w)
    cat = jnp.concatenate([dn.reshape(B, L, DN_V_DIM), fx.reshape(B, L, FOX_DIM).astype(dn.dtype)], axis=-1)
    return cat @ w_out


def moe_ffn(h, layer, w_router, b_router, w_gate, b_gate, w_up, b_up, w_down, b_down):
    B, L, D = h.shape
    t = h.reshape(B * L, D)
    n_tok = t.shape[0]
    logits = (t @ w_router[layer] + b_router[layer]).astype(jnp.float32)
    top_logit, top_idx = lax.top_k(logits, TOP_K)
    top_w = jax.nn.softmax(top_logit, axis=-1)
    n_asg = n_tok * TOP_K
    exp_flat = top_idx.reshape(-1)
    order = jnp.argsort(exp_flat)
    exp_sorted = exp_flat[order]
    tok_sorted = (order // TOP_K).astype(jnp.int32)
    counts = jnp.bincount(exp_flat, length=N_EXP)
    start = jnp.cumsum(counts) - counts
    padded = (counts + MOE_BLOCK - 1) // MOE_BLOCK * MOE_BLOCK
    pend = jnp.cumsum(padded)
    pstart = pend - padded
    dest = pstart[exp_sorted] + (jnp.arange(n_asg) - start[exp_sorted])
    n_blocks = -(-n_asg // MOE_BLOCK) + N_EXP
    n_rows = n_blocks * MOE_BLOCK
    row_tok = jnp.full((n_rows,), n_tok, jnp.int32).at[dest].set(tok_sorted)
    row_w = jnp.zeros((n_rows,), jnp.float32).at[dest].set(top_w.reshape(-1)[order])
    block_exp = jnp.minimum(jnp.searchsorted(pend, jnp.arange(n_blocks) * MOE_BLOCK, side='right'), N_EXP - 1)
    t_pad = jnp.concatenate([t, jnp.zeros((1, D), t.dtype)], axis=0)

    def block_ffn(xs):
        rows, e = xs
        xb = t_pad[rows]
        gt = jnp.minimum(xb @ w_gate[layer, e] + b_gate[layer, e], SWIGLU_LIMIT)
        up = jnp.clip(xb @ w_up[layer, e] + b_up[layer, e], -SWIGLU_LIMIT, SWIGLU_LIMIT)
        act = gt * jax.nn.sigmoid(SWIGLU_ALPHA * gt) * (up + 1.0)
        return act @ w_down[layer, e] + b_down[layer, e]

    out = lax.map(block_ffn, (row_tok.reshape(n_blocks, MOE_BLOCK), block_exp))
    out = out.reshape(n_rows, D) * row_w[:, None].astype(out.dtype)
    y = jnp.zeros((n_tok + 1, D), out.dtype).at[row_tok].add(out)[:n_tok]
    return y.reshape(B, L, D)


def setup_inputs(seed: int = 0) -> dict:
    key = jax.random.key(seed)
    ks = jax.random.split(key, 28)
    f32 = jnp.float32
    n_pages = PAST_LEN // PAGE_SIZE
    n_used = DEC_BATCH * n_pages
    n_phys = n_used + max(1, n_used // 4)

    def nrm(k, shape, scale):
        return jax.random.normal(k, shape, f32) * scale

    page_table = jax.random.permutation(ks[7], n_phys)[:n_used].reshape(DEC_BATCH, n_pages).astype(jnp.int32)
    return {
        'x_prompt': nrm(ks[0], (BATCH, SEQ, D_MODEL), 1.0),
        'x_sample': nrm(ks[1], (DEC_BATCH, DEC_SEQ, D_MODEL), 1.0),
        'cache_fox_k': nrm(ks[2], (DEPTH, n_phys, PAGE_SIZE, H_FOX, HEAD_DIM), 1.0),
        'cache_fox_v': nrm(ks[3], (DEPTH, n_phys, PAGE_SIZE, H_FOX, HEAD_DIM), 1.0),
        'cache_fox_logf': jax.nn.log_sigmoid(2.0 + nrm(ks[4], (DEPTH, n_phys, PAGE_SIZE, H_FOX), 1.0)),
        'state_dn': nrm(ks[5], (DEPTH, DEC_BATCH, H_DN, DN_DK, DN_DV), 0.1),
        'state_dn_conv': nrm(ks[6], (DEPTH, DEC_BATCH, CONV_W - 1, DN_CONV_DIM), 1.0),
        'page_table': page_table,
        'attn_norm_w': 1.0 + nrm(ks[8], (DEPTH, D_MODEL), 0.02),
        'w_in': nrm(ks[9], (DEPTH, D_MODEL, IN_DIM), D_MODEL ** -0.5),
        'dn_conv_w': nrm(ks[10], (DEPTH, CONV_W, DN_CONV_DIM), CONV_W ** -0.5),
        'dn_a_log': jnp.log(jax.random.uniform(ks[11], (DEPTH, H_DN), f32, 1.0, 16.0)),
        'dn_dt_bias': nrm(ks[12], (DEPTH, H_DN), 0.1),
        'fox_f_bias': jax.random.uniform(ks[13], (DEPTH, H_FOX), f32, 1.0, 3.0),
        'dn_norm_w': 1.0 + nrm(ks[14], (DEPTH, DN_DV), 0.02),
        'fox_norm_w': 1.0 + nrm(ks[15], (DEPTH, HEAD_DIM), 0.02),
        'w_out': nrm(ks[16], (DEPTH, MIX_DIM, D_MODEL), MIX_DIM ** -0.5),
        'ffn_norm_w': 1.0 + nrm(ks[17], (DEPTH, D_MODEL), 0.02),
        'w_router': nrm(ks[18], (DEPTH, D_MODEL, N_EXP), D_MODEL ** -0.5),
        'b_router': nrm(ks[19], (DEPTH, N_EXP), 0.01),
        'w_gate': nrm(ks[20], (DEPTH, N_EXP, D_MODEL, D_FF), D_MODEL ** -0.5),
        'b_gate': nrm(ks[21], (DEPTH, N_EXP, D_FF), 0.02),
        'w_up': nrm(ks[22], (DEPTH, N_EXP, D_MODEL, D_FF), D_MODEL ** -0.5),
        'b_up': nrm(ks[23], (DEPTH, N_EXP, D_FF), 0.02),
        'w_down': nrm(ks[24], (DEPTH, N_EXP, D_FF, D_MODEL), D_FF ** -0.5),
        'b_down': nrm(ks[25], (DEPTH, N_EXP, D_MODEL), 0.02),
        'final_norm_w': 1.0 + nrm(ks[26], (D_MODEL,), 0.02),
    }


def reference(x_prompt, x_sample, cache_fox_k, cache_fox_v, cache_fox_logf, state_dn, state_dn_conv, page_table,
              attn_norm_w, w_in, dn_conv_w, dn_a_log, dn_dt_bias, fox_f_bias, dn_norm_w, fox_norm_w, w_out,
              ffn_norm_w, w_router, b_router, w_gate, b_gate, w_up, b_up, w_down, b_down, final_norm_w):
    xp = x_prompt
    xs = x_sample
    Bp, Lp, _ = xp.shape
    pk, pv, plf, pst, pcv = [], [], [], [], []
    sk, sv, slf, sst, scv = [], [], [], [], []
    for l in range(DEPTH):
        h = rmsnorm(xp, attn_norm_w[l])
        zero_buf = jnp.zeros((Bp, CONV_W - 1, DN_CONV_DIM), xp.dtype)
        zero_s = jnp.zeros((Bp, H_DN, DN_DK, DN_DV), state_dn.dtype)
        (dq, dk, dv, beta, g, z, buf), (fq, fk, fv, lf) = mix_inputs(
            h, zero_buf, w_in[l], dn_conv_w[l], dn_a_log[l], dn_dt_bias[l], fox_f_bias[l])
        o_dn, s_fin = gated_delta_rule(dq, dk, dv, g, beta, zero_s)
        o_fox = fox_prompt_attention(fq, fk, fv, lf)
        xp = xp + mix_output(o_dn, z, o_fox, dn_norm_w[l], fox_norm_w[l], w_out[l])
        xp = xp + moe_ffn(rmsnorm(xp, ffn_norm_w[l]), l, w_router, b_router, w_gate, b_gate, w_up, b_up, w_down, b_down)
        npg = Lp // PAGE_SIZE
        pk.append(fk.reshape(Bp, npg, PAGE_SIZE, H_FOX, HEAD_DIM).astype(cache_fox_k.dtype))
        pv.append(fv.reshape(Bp, npg, PAGE_SIZE, H_FOX, HEAD_DIM).astype(cache_fox_v.dtype))
        plf.append(lf.reshape(Bp, npg, PAGE_SIZE, H_FOX).astype(cache_fox_logf.dtype))
        pst.append(s_fin)
        pcv.append(buf.astype(state_dn_conv.dtype))
        h = rmsnorm(xs, attn_norm_w[l])
        (dq, dk, dv, beta, g, z, buf), (fq, fk, fv, lf) = mix_inputs(
            h, state_dn_conv[l], w_in[l], dn_conv_w[l], dn_a_log[l], dn_dt_bias[l], fox_f_bias[l])
        o_dn, s_fin = gated_delta_rule(dq, dk, dv, g, beta, state_dn[l])
        o_fox = fox_sample_attention(fq, fk, fv, lf, l, cache_fox_k, cache_fox_v, cache_fox_logf, page_table)
        xs = xs + mix_output(o_dn, z, o_fox, dn_norm_w[l], fox_norm_w[l], w_out[l])
        xs = xs + moe_ffn(rmsnorm(xs, ffn_norm_w[l]), l, w_router, b_router, w_gate, b_gate, w_up, b_up, w_down, b_down)
        sk.append(fk.astype(cache_fox_k.dtype))
        sv.append(fv.astype(cache_fox_v.dtype))
        slf.append(lf.astype(cache_fox_logf.dtype))
        sst.append(s_fin)
        scv.append(buf.astype(state_dn_conv.dtype))
    y_prompt = rmsnorm(xp, final_norm_w)
    y_sample = rmsnorm(xs, final_norm_w)
    new_fox_k_prompt = jnp.stack(pk)
    new_fox_v_prompt = jnp.stack(pv)
    new_fox_logf_prompt = jnp.stack(plf)
    new_dn_state_prompt = jnp.stack(pst)
    new_dn_conv_prompt = jnp.stack(pcv)
    new_fox_k_sample = jnp.stack(sk)
    new_fox_v_sample = jnp.stack(sv)
    new_fox_logf_sample = jnp.stack(slf)
    new_dn_state_sample = jnp.stack(sst)
    new_dn_conv_sample = jnp.stack(scv)
    return (y_prompt, y_sample, new_fox_k_prompt, new_fox_v_prompt, new_fox_logf_prompt, new_dn_state_prompt,
            new_dn_conv_prompt, new_fox_k_sample, new_fox_v_sample, new_fox_logf_sample, new_dn_state_sample,
            new_dn_conv_sample)
```

```python
import functools
import math

import jax
import jax.numpy as jnp
from jax import lax
from jax.experimental import pallas as pl
from jax.experimental.pallas import tpu as pltpu

F32 = jnp.float32
BF16 = jnp.bfloat16
I32 = jnp.int32

HEAD_DIM = 128
N_DN_HEADS = 16
N_FOX_HEADS = 16
CONV_W = 4
DN_CHUNK = 64
N_EXP = 32
TOP_K = 4
SWIGLU_LIMIT = 7.0
SWIGLU_ALPHA = 1.702
RMS_EPS = 1e-6
L2_EPS = 1e-6

LANE_BETA = 0
LANE_G = 16
LANE_LOGF = 32

VMEM_LIMIT = 56 * 1024 * 1024
MOE_ROWS = 256
MOE_COLS = 512
NEG_INF = float("-inf")


def _cparams(*sem):
    return pltpu.CompilerParams(dimension_semantics=sem, vmem_limit_bytes=VMEM_LIMIT)


def _rmsnorm_kernel(x_ref, w_ref, o_ref):
    x = x_ref[...]
    y = x * lax.rsqrt(jnp.mean(x * x, axis=-1, keepdims=True) + RMS_EPS)
    o_ref[...] = (y * w_ref[...]).astype(o_ref.dtype)


def rmsnorm_rows(x, w, out_dtype, tr):
    t, d = x.shape
    return pl.pallas_call(
        _rmsnorm_kernel,
        grid=(t // tr,),
        in_specs=[pl.BlockSpec((tr, d), lambda i: (i, 0)), pl.BlockSpec((1, d), lambda i: (0, 0))],
        out_specs=pl.BlockSpec((tr, d), lambda i: (i, 0)),
        out_shape=jax.ShapeDtypeStruct((t, d), out_dtype),
        compiler_params=_cparams("parallel"),
        name="rmsnorm",
    )(x, w.reshape(1, d))


def _mm_kernel(a_ref, b_ref, *rest, has_res):
    o_ref = rest[-1]
    acc = jnp.dot(a_ref[...], b_ref[...], preferred_element_type=F32)
    if has_res:
        acc = acc + rest[0][...]
    o_ref[...] = acc


def matmul(a, b, res=None, *, tm, tn, name):
    m, k = a.shape
    n = b.shape[1]
    in_specs = [pl.BlockSpec((tm, k), lambda j, i: (i, 0)), pl.BlockSpec((k, tn), lambda j, i: (0, j))]
    args = [a, b]
    if res is not None:
        in_specs.append(pl.BlockSpec((tm, tn), lambda j, i: (i, j)))
        args.append(res)
    return pl.pallas_call(
        functools.partial(_mm_kernel, has_res=res is not None),
        grid=(n // tn, m // tm),
        in_specs=in_specs,
        out_specs=pl.BlockSpec((tm, tn), lambda j, i: (i, j)),
        out_shape=jax.ShapeDtypeStruct((m, n), F32),
        compiler_params=_cparams("parallel", "parallel"),
        name=name,
    )(*args)


def _softplus(x):
    return jnp.maximum(x, 0.0) + jnp.log1p(jnp.exp(-jnp.abs(x)))


def _gates_kernel(s_ref, p_ref, g1_ref, g2_ref, *, seg_g, seg_f):
    x = s_ref[...]
    lane = lax.broadcasted_iota(I32, x.shape, 1)
    row = lax.broadcasted_iota(I32, x.shape, 0)
    is_b = lane < LANE_G
    is_g = jnp.logical_and(lane >= LANE_G, lane < LANE_LOGF)
    is_f = jnp.logical_and(lane >= LANE_LOGF, lane < LANE_LOGF + N_FOX_HEADS)
    beta = jax.nn.sigmoid(x)
    g = -jnp.exp(p_ref[0:1, :]) * _softplus(x + p_ref[1:2, :])
    lf = -_softplus(-(x + p_ref[2:3, :]))
    g1 = jnp.where(is_b, beta, jnp.where(is_g, g, jnp.where(is_f, lf, 0.0)))
    g1_ref[...] = g1
    c = jnp.where(is_b, 0.0, g1)
    pos = jnp.where(is_g, row & (seg_g - 1), row & (seg_f - 1))
    s = 1
    while s < max(seg_g, seg_f):
        c = c + jnp.where(pos >= s, pltpu.roll(c, s, axis=0), 0.0)
        s *= 2
    g2_ref[...] = c


def gates(small, params, *, rows_blk, seg_g, seg_f):
    t = small.shape[0]
    spec = pl.BlockSpec((rows_blk, 128), lambda i: (i, 0))
    return pl.pallas_call(
        functools.partial(_gates_kernel, seg_g=seg_g, seg_f=seg_f),
        grid=(t // rows_blk,),
        in_specs=[spec, pl.BlockSpec((8, 128), lambda i: (0, 0))],
        out_specs=[spec, spec],
        out_shape=[jax.ShapeDtypeStruct((t, 128), F32)] * 2,
        compiler_params=_cparams("parallel"),
        name="gates",
    )(small, params)


def _bmm(a, b):
    return lax.dot_general(a, b, (((2,), (1,)), ((0,), (0,))), preferred_element_type=F32)


def _bmm_nt(a, b):
    return lax.dot_general(a, b, (((2,), (2,)), ((0,), (0,))), preferred_element_type=F32)


def _bmm_f32(a, b):
    return lax.dot_general(a, b, (((2,), (1,)), ((0,), (0,))), precision=lax.Precision.HIGHEST,
                           preferred_element_type=F32)


def _gdn_kernel(q_ref, k_ref, v_ref, z_ref, cwq_ref, cwk_ref, cwv_ref, g1_ref, g2_ref, gr_ref, nw_ref, *rest,
                chunk, n_chunks, pad_rows, has_s0):
    if has_s0:
        s0_ref, o_ref, sf_ref, u_s, w_s, qs_s, kd_s, qk_s, el_s, o_s = rest
    else:
        o_ref, sf_ref, u_s, w_s, qs_s, kd_s, qk_s, el_s, o_s = rest
    c, n = chunk, n_chunks
    rows = c * n
    dk = HEAD_DIM
    h = pl.program_id(1)
    row = lax.broadcasted_iota(I32, (rows, dk), 0)
    lane = lax.broadcasted_iota(I32, (rows, dk), 1)

    def conv_silu(x_ref, cw_ref):
        x = x_ref[...]
        acc = x * cw_ref[CONV_W - 1:CONV_W, :]
        for j in range(CONV_W - 1):
            sh = CONV_W - 1 - j
            acc = acc + jnp.where(row >= sh, pltpu.roll(x, sh, axis=0), 0.0) * cw_ref[j:j + 1, :]
        y = acc * jax.nn.sigmoid(acc)
        if pad_rows:
            y = jnp.where(row >= pad_rows, y, 0.0)
        return y

    def l2n(x):
        return x * lax.rsqrt(jnp.sum(x * x, axis=-1, keepdims=True) + L2_EPS)

    q = l2n(conv_silu(q_ref, cwq_ref))
    k = l2n(conv_silu(k_ref, cwk_ref))
    v = conv_silu(v_ref, cwv_ref)
    beta = jnp.sum(jnp.where(lane == LANE_BETA + h, g1_ref[...], 0.0), axis=1, keepdims=True)
    gc = jnp.sum(jnp.where(lane == LANE_G + h, g2_ref[...], 0.0), axis=1, keepdims=True)

    q3 = q.reshape(n, c, dk)
    k3 = k.reshape(n, c, dk)
    v3 = v.reshape(n, c, dk)
    beta3 = beta.reshape(n, c, 1)
    gc3 = gc.reshape(n, c, 1)
    gr3 = gr_ref[...]
    ii = lax.broadcasted_iota(I32, (c, c), 0)
    jj = lax.broadcasted_iota(I32, (c, c), 1)
    incl = (ii >= jj)[None]
    strict = (ii > jj)[None]
    decay = jnp.where(incl, jnp.exp(jnp.where(incl, gc3 - gr3, 0.0)), 0.0)
    kb = k3 * beta3
    k3b = k3.astype(BF16)
    a = jnp.where(strict, _bmm_nt(kb.astype(BF16), k3b) * decay, 0.0)
    tinv = jnp.where((ii == jj)[None], 1.0, 0.0) - a
    x = a
    p = 2
    while p < c:
        x = _bmm_f32(x, x)
        tinv = tinv + _bmm_f32(tinv, x)
        p *= 2
    tb = tinv.astype(BF16)
    u_s[...] = _bmm(tb, (v3 * beta3).astype(BF16))
    w_s[...] = _bmm(tb, (kb * jnp.exp(gc3)).astype(BF16))
    scale = dk ** -0.5
    qsc = q3 * scale
    qk_s[...] = _bmm_nt(qsc.astype(BF16), k3b) * decay
    qs_s[...] = qsc * jnp.exp(gc3)
    gl3 = gc3[:, c - 1:c, :]
    kd_s[...] = k3 * jnp.exp(gl3 - gc3)
    el_s[...] = jnp.broadcast_to(jnp.exp(gl3), (n, 1, dk))

    def step(i, s):
        sb = s.astype(BF16)
        v_new = u_s[i] - jnp.dot(w_s[i].astype(BF16), sb, preferred_element_type=F32)
        vb = v_new.astype(BF16)
        o_s[i] = (jnp.dot(qs_s[i].astype(BF16), sb, preferred_element_type=F32)
                  + jnp.dot(qk_s[i].astype(BF16), vb, preferred_element_type=F32))
        return s * el_s[i] + lax.dot_general(kd_s[i].astype(BF16), vb, (((0,), (0,)), ((), ())),
                                             preferred_element_type=F32)

    s_init = s0_ref[...] if has_s0 else jnp.zeros((dk, dk), F32)
    if n == 1:
        s_fin = step(0, s_init)
    else:
        s_fin = lax.fori_loop(0, n, step, s_init)
    sf_ref[...] = s_fin
    o = o_s[...].reshape(rows, dk)
    on = o * lax.rsqrt(jnp.mean(o * o, axis=-1, keepdims=True) + RMS_EPS) * nw_ref[...]
    z = z_ref[...]
    o_ref[...] = on * (z * jax.nn.sigmoid(z))


def gdn_heads(xq, xk, xv, xz, col_off, conv_w, g1, g2, gr, norm_w, s0, *, n_seq, chunk, n_chunks, pad_rows):
    rows = chunk * n_chunks
    hh = N_DN_HEADS
    qo, ko, vo, zo = col_off

    def xspec(off):
        return pl.BlockSpec((rows, HEAD_DIM), lambda b, h, off=off: (b, off + h))

    def cwspec(off):
        return pl.BlockSpec((CONV_W, HEAD_DIM), lambda b, h, off=off: (0, off + h))

    gspec = pl.BlockSpec((rows, 128), lambda b, h: (b, 0))
    in_specs = [xspec(qo), xspec(ko), xspec(vo), xspec(zo), cwspec(0), cwspec(hh), cwspec(2 * hh), gspec, gspec,
                pl.BlockSpec((None, None, n_chunks, 1, chunk), lambda b, h: (b, h, 0, 0, 0)),
                pl.BlockSpec((1, HEAD_DIM), lambda b, h: (0, 0))]
    args = [xq, xk, xv, xz, conv_w, conv_w, conv_w, g1, g2, gr, norm_w]
    if s0 is not None:
        in_specs.append(pl.BlockSpec((None, None, HEAD_DIM, HEAD_DIM), lambda b, h: (b, h, 0, 0)))
        args.append(s0)
    big = pltpu.VMEM((n_chunks, chunk, HEAD_DIM), F32)
    return pl.pallas_call(
        functools.partial(_gdn_kernel, chunk=chunk, n_chunks=n_chunks, pad_rows=pad_rows, has_s0=s0 is not None),
        grid=(n_seq, hh),
        in_specs=in_specs,
        out_specs=[pl.BlockSpec((rows, HEAD_DIM), lambda b, h: (b, h)),
                   pl.BlockSpec((None, None, HEAD_DIM, HEAD_DIM), lambda b, h: (b, h, 0, 0))],
        out_shape=[jax.ShapeDtypeStruct((n_seq * rows, hh * HEAD_DIM), F32),
                   jax.ShapeDtypeStruct((n_seq, hh, HEAD_DIM, HEAD_DIM), F32)],
        scratch_shapes=[big, big, big, big, pltpu.VMEM((n_chunks, chunk, chunk), F32),
                        pltpu.VMEM((n_chunks, 1, HEAD_DIM), F32), big],
        compiler_params=_cparams("parallel", "parallel"),
        name="gdn_c%d" % chunk,
    )(*args)


def _fox_prompt_kernel(q_ref, k_ref, v_ref, g2_ref, ck_ref, nw_ref, o_ref, *, tq, tk):
    h = pl.program_id(1)
    i = pl.program_id(2)
    d = HEAD_DIM
    scale = d ** -0.5
    q = q_ref[...].astype(BF16)
    lane = lax.broadcasted_iota(I32, (tq, 128), 1)
    cq = jnp.sum(jnp.where(lane == LANE_LOGF + h, g2_ref[...], 0.0), axis=1, keepdims=True)
    qpos = i * tq + lax.broadcasted_iota(I32, (tq, tk), 0)
    kloc = lax.broadcasted_iota(I32, (tq, tk), 1)

    def body(j, carry):
        m, l, acc = carry
        start = pl.multiple_of(j * tk, tk)
        kj = k_ref[pl.ds(start, tk), :].astype(BF16)
        vj = v_ref[pl.ds(start, tk), :].astype(BF16)
        s = lax.dot_general(q, kj, (((1,), (1,)), ((), ())), preferred_element_type=F32) * scale
        s = s + cq - ck_ref[j]
        s = jnp.where(qpos >= start + kloc, s, NEG_INF)
        m_new = jnp.maximum(m, jnp.max(s, axis=1, keepdims=True))
        p = jnp.exp(s - m_new)
        corr = jnp.exp(m - m_new)
        l = l * corr + jnp.sum(p, axis=1, keepdims=True)
        acc = acc * corr + jnp.dot(p.astype(BF16), vj, preferred_element_type=F32)
        return m_new, l, acc

    n_kv = ((i + 1) * tq + tk - 1) // tk
    init = (jnp.full((tq, 1), NEG_INF, F32), jnp.zeros((tq, 1), F32), jnp.zeros((tq, d), F32))
    _, l, acc = lax.fori_loop(0, n_kv, body, init)
    o = acc / l
    o_ref[...] = o * lax.rsqrt(jnp.mean(o * o, axis=-1, keepdims=True) + RMS_EPS) * nw_ref[...]


def fox_prompt(proj, col_off, g2, ck, norm_w, *, n_seq, seq, tq, tk):
    qo, ko, vo = col_off
    hh = N_FOX_HEADS
    nq = seq // tq
    return pl.pallas_call(
        functools.partial(_fox_prompt_kernel, tq=tq, tk=tk),
        grid=(n_seq, hh, nq),
        in_specs=[pl.BlockSpec((tq, HEAD_DIM), lambda b, h, i: (b * nq + i, qo + h)),
                  pl.BlockSpec((seq, HEAD_DIM), lambda b, h, i: (b, ko + h)),
                  pl.BlockSpec((seq, HEAD_DIM), lambda b, h, i: (b, vo + h)),
                  pl.BlockSpec((tq, 128), lambda b, h, i: (b * nq + i, 0)),
                  pl.BlockSpec((None, None, seq // tk, 1, tk), lambda b, h, i: (b, h, 0, 0, 0)),
                  pl.BlockSpec((1, HEAD_DIM), lambda b, h, i: (0, 0))],
        out_specs=pl.BlockSpec((tq, HEAD_DIM), lambda b, h, i: (b * nq + i, h)),
        out_shape=jax.ShapeDtypeStruct((n_seq * seq, hh * HEAD_DIM), F32),
        compiler_params=_cparams("parallel", "parallel", "arbitrary"),
        name="fox_prompt",
    )(proj, proj, proj, g2, ck, norm_w)


def _fox_cpast_kernel(pt_ref, lf_ref, o_ref, carry):
    del pt_ref
    p = pl.program_id(1)

    @pl.when(p == 0)
    def _():
        carry[...] = jnp.zeros_like(carry)

    lf = lf_ref[...]
    ps = lf.shape[0]
    ii = lax.broadcasted_iota(I32, (ps, ps), 0)
    jj = lax.broadcasted_iota(I32, (ps, ps), 1)
    later = jnp.where(jj > ii, 1.0, 0.0)
    excl = jnp.dot(later, lf, precision=lax.Precision.HIGHEST, preferred_element_type=F32)
    o_ref[...] = -(excl + carry[...])
    carry[...] = carry[...] + jnp.sum(lf, axis=0, keepdims=True)


def fox_cpast(page_table_flat, cache_logf, *, n_seq, n_pages):
    _, _, ps, hh = cache_logf.shape
    grid_spec = pltpu.PrefetchScalarGridSpec(
        num_scalar_prefetch=1,
        grid=(n_seq, n_pages),
        in_specs=[pl.BlockSpec((None, None, ps, hh),
                               lambda b, p, pt: (0, pt[b * n_pages + n_pages - 1 - p], 0, 0))],
        out_specs=pl.BlockSpec((None, None, ps, hh), lambda b, p, pt: (b, n_pages - 1 - p, 0, 0)),
        scratch_shapes=[pltpu.VMEM((1, hh), F32)],
    )
    return pl.pallas_call(
        _fox_cpast_kernel,
        grid_spec=grid_spec,
        out_shape=jax.ShapeDtypeStruct((n_seq, n_pages, ps, hh), F32),
        compiler_params=_cparams("parallel", "arbitrary"),
        name="fox_cpast",
    )(page_table_flat, cache_logf)


def _fox_sample_kernel(pt_ref, q_ref, k_ref, v_ref, cpg_ref, cq_ref, kn_ref, vn_ref, cn_ref, nw_ref, o_ref,
                       m_s, l_s, acc_s, *, n_pages, n_new):
    del pt_ref
    p = pl.program_id(1)
    d = HEAD_DIM
    hh = N_FOX_HEADS
    nr = n_new * hh

    @pl.when(p == 0)
    def _():
        m_s[...] = jnp.full_like(m_s, NEG_INF)
        l_s[...] = jnp.zeros_like(l_s)
        acc_s[...] = jnp.zeros_like(acc_s)

    q = (q_ref[...] * (d ** -0.5)).astype(BF16)
    cq = cq_ref[...]

    def update(s, vals):
        m = m_s[...]
        m_new = jnp.maximum(m, jnp.max(s, axis=1, keepdims=True))
        pr = jnp.exp(s - m_new)
        corr = jnp.exp(m - m_new)
        l_s[...] = l_s[...] * corr + jnp.sum(pr, axis=1, keepdims=True)
        acc_s[...] = acc_s[...] * corr + jnp.dot(pr.astype(BF16), vals, preferred_element_type=F32)
        m_s[...] = m_new

    ps = k_ref.shape[0]
    nc = ps * hh
    kf = k_ref[...].reshape(nc, d).astype(BF16)
    vf = v_ref[...].reshape(nc, d).astype(BF16)
    s = lax.dot_general(q, kf, (((1,), (1,)), ((), ())), preferred_element_type=F32)
    s = s + cq - cpg_ref[...]
    rh = lax.broadcasted_iota(I32, (nr, nc), 0) & (hh - 1)
    ch = lax.broadcasted_iota(I32, (nr, nc), 1) & (hh - 1)
    update(jnp.where(rh == ch, s, NEG_INF), vf)

    @pl.when(p == n_pages - 1)
    def _():
        kn = kn_ref[...].astype(BF16)
        vn = vn_ref[...].astype(BF16)
        s2 = lax.dot_general(q, kn, (((1,), (1,)), ((), ())), preferred_element_type=F32)
        s2 = s2 + cq - cn_ref[...]
        r = lax.broadcasted_iota(I32, (nr, nr), 0)
        c = lax.broadcasted_iota(I32, (nr, nr), 1)
        keep = jnp.logical_and((r & (hh - 1)) == (c & (hh - 1)), r >= c)
        update(jnp.where(keep, s2, NEG_INF), vn)
        o = acc_s[...] / l_s[...]
        o_ref[...] = o * lax.rsqrt(jnp.mean(o * o, axis=-1, keepdims=True) + RMS_EPS) * nw_ref[...]


def fox_sample(page_table_flat, q, cache_k, cache_v, cpg, cq, k_new, v_new, c_new, norm_w, *, n_seq, n_pages, n_new):
    _, _, ps, hh, d = cache_k.shape
    nr = n_new * hh
    cache_spec = pl.BlockSpec((None, None, ps, hh, d), lambda b, p, pt: (0, pt[b * n_pages + p], 0, 0, 0))
    seq_spec = pl.BlockSpec((None, nr, d), lambda b, p, pt: (b, 0, 0))
    grid_spec = pltpu.PrefetchScalarGridSpec(
        num_scalar_prefetch=1,
        grid=(n_seq, n_pages),
        in_specs=[seq_spec, cache_spec, cache_spec,
                  pl.BlockSpec((None, 1, ps * hh), lambda b, p, pt: (b * n_pages + p, 0, 0)),
                  pl.BlockSpec((None, nr, 1), lambda b, p, pt: (b, 0, 0)),
                  seq_spec, seq_spec,
                  pl.BlockSpec((None, 1, nr), lambda b, p, pt: (b, 0, 0)),
                  pl.BlockSpec((1, d), lambda b, p, pt: (0, 0))],
        out_specs=seq_spec,
        scratch_shapes=[pltpu.VMEM((nr, 1), F32), pltpu.VMEM((nr, 1), F32), pltpu.VMEM((nr, d), F32)],
    )
    return pl.pallas_call(
        functools.partial(_fox_sample_kernel, n_pages=n_pages, n_new=n_new),
        grid_spec=grid_spec,
        out_shape=jax.ShapeDtypeStruct((n_seq, nr, d), F32),
        compiler_params=_cparams("parallel", "arbitrary"),
        name="fox_sample",
    )(page_table_flat, q, cache_k, cache_v, cpg, cq, k_new, v_new, c_new, norm_w)


def _router_kernel(x_ref, nw_ref, wr_ref, br_ref, idx_ref, w_ref):
    x = x_ref[...]
    hcur = x * lax.rsqrt(jnp.mean(x * x, axis=-1, keepdims=True) + RMS_EPS) * nw_ref[...]
    logits = jnp.dot(hcur, wr_ref[...], precision=lax.Precision.HIGHEST, preferred_element_type=F32) + br_ref[...]
    tr, ne = logits.shape
    lane = lax.broadcasted_iota(I32, (tr, ne), 1)
    lane_f = lane.astype(F32)
    olane = lax.broadcasted_iota(I32, (tr, 128), 1)
    idx_out = jnp.zeros((tr, 128), I32)
    w_out = jnp.zeros((tr, 128), F32)
    cur = logits
    tops = []
    for kk in range(TOP_K):
        mx = jnp.max(cur, axis=1, keepdims=True)
        sel = jnp.min(jnp.where(cur == mx, lane_f, float(ne)), axis=1, keepdims=True).astype(I32)
        tops.append(mx)
        idx_out = jnp.where(olane == kk, sel, idx_out)
        cur = jnp.where(lane == sel, NEG_INF, cur)
    es = [jnp.exp(t - tops[0]) for t in tops]
    tot = es[0]
    for e in es[1:]:
        tot = tot + e
    for kk in range(TOP_K):
        w_out = jnp.where(olane == kk, es[kk] / tot, w_out)
    idx_ref[...] = idx_out
    w_ref[...] = w_out


def router(x, norm_w, w_router, b_router, *, tr):
    t, d = x.shape
    ne = w_router.shape[1]
    return pl.pallas_call(
        _router_kernel,
        grid=(t // tr,),
        in_specs=[pl.BlockSpec((tr, d), lambda i: (i, 0)), pl.BlockSpec((1, d), lambda i: (0, 0)),
                  pl.BlockSpec((d, ne), lambda i: (0, 0)), pl.BlockSpec((1, ne), lambda i: (0, 0))],
        out_specs=[pl.BlockSpec((tr, 128), lambda i: (i, 0))] * 2,
        out_shape=[jax.ShapeDtypeStruct((t, 128), I32), jax.ShapeDtypeStruct((t, 128), F32)],
        compiler_params=_cparams("parallel"),
        name="router",
    )(x, norm_w.reshape(1, d), w_router, b_router.reshape(1, ne))


def _row_copy(src_hbm, dst, sem, src_row, dst_row):
    return pltpu.make_async_copy(src_hbm.at[pl.ds(src_row, 1), :], dst.at[pl.ds(dst_row, 1), :], sem)


def _moe_gather_kernel(tok_ref, x_hbm, nw_ref, o_ref, buf, sem, *, tg):
    i = pl.program_id(0)
    n = pl.num_programs(0)

    def start_block(blk, slot):
        def issue(r, carry):
            _row_copy(x_hbm, buf.at[slot], sem.at[slot], tok_ref[blk * tg + r], r).start()
            return carry
        lax.fori_loop(0, tg, issue, 0)

    @pl.when(i == 0)
    def _():
        start_block(0, 0)

    @pl.when(i + 1 < n)
    def _():
        start_block(i + 1, (i + 1) % 2)

    slot = i % 2

    def wait(r, carry):
        _row_copy(x_hbm, buf.at[slot], sem.at[slot], 0, r).wait()
        return carry
    lax.fori_loop(0, tg, wait, 0)
    x = buf[slot]
    o_ref[...] = (x * lax.rsqrt(jnp.mean(x * x, axis=-1, keepdims=True) + RMS_EPS) * nw_ref[...]).astype(o_ref.dtype)


def moe_gather(row_tok, x, norm_w, *, tg):
    n_rows = row_tok.shape[0]
    d = x.shape[1]
    grid_spec = pltpu.PrefetchScalarGridSpec(
        num_scalar_prefetch=1,
        grid=(n_rows // tg,),
        in_specs=[pl.BlockSpec(memory_space=pl.ANY), pl.BlockSpec((1, d), lambda i, tok: (0, 0))],
        out_specs=pl.BlockSpec((tg, d), lambda i, tok: (i, 0)),
        scratch_shapes=[pltpu.VMEM((2, tg, d), F32), pltpu.SemaphoreType.DMA((2,))],
    )
    return pl.pallas_call(
        functools.partial(_moe_gather_kernel, tg=tg),
        grid_spec=grid_spec,
        out_shape=jax.ShapeDtypeStruct((n_rows, d), BF16),
        compiler_params=_cparams("arbitrary"),
        name="moe_gather",
    )(row_tok, x, norm_w.reshape(1, d))


def _moe_up_kernel(rb_ref, e_ref, j_ref, ok_ref, x_ref, wg_ref, wu_ref, bg_ref, bu_ref, o_ref):
    del rb_ref, e_ref, j_ref
    i = pl.program_id(0)

    @pl.when(ok_ref[i] == 1)
    def _():
        x = x_ref[...]
        g = jnp.dot(x, wg_ref[...].astype(BF16), preferred_element_type=F32) + bg_ref[...]
        u = jnp.dot(x, wu_ref[...].astype(BF16), preferred_element_type=F32) + bu_ref[...]
        gt = jnp.minimum(g, SWIGLU_LIMIT)
        up = jnp.clip(u, -SWIGLU_LIMIT, SWIGLU_LIMIT)
        o_ref[...] = (gt * jax.nn.sigmoid(SWIGLU_ALPHA * gt) * (up + 1.0)).astype(o_ref.dtype)

    @pl.when(ok_ref[i] == 0)
    def _():
        o_ref[...] = jnp.zeros_like(o_ref)


def _moe_down_kernel(rb_ref, e_ref, j_ref, ok_ref, x_ref, wd_ref, bd_ref, o_ref):
    del rb_ref, e_ref, j_ref
    i = pl.program_id(0)

    @pl.when(ok_ref[i] == 1)
    def _():
        o_ref[...] = jnp.dot(x_ref[...], wd_ref[...].astype(BF16), preferred_element_type=F32) + bd_ref[...]

    @pl.when(ok_ref[i] == 0)
    def _():
        o_ref[...] = jnp.zeros_like(o_ref)


def moe_grouped(sched, x, weights, biases, *, out_dtype, kern, name):
    n_rows, d = x.shape
    n_items = sched[0].shape[0]
    dout = weights[0].shape[-1]
    tm, tc = MOE_ROWS, MOE_COLS
    wspec = pl.BlockSpec((None, None, d, tc), lambda i, rb, e, j, ok: (0, e[i], 0, j[i]))
    bspec = pl.BlockSpec((None, 1, tc), lambda i, rb, e, j, ok: (e[i], 0, j[i]))
    grid_spec = pltpu.PrefetchScalarGridSpec(
        num_scalar_prefetch=4,
        grid=(n_items,),
        in_specs=[pl.BlockSpec((tm, d), lambda i, rb, e, j, ok: (rb[i], 0))]
        + [wspec] * len(weights) + [bspec] * len(biases),
        out_specs=pl.BlockSpec((tm, tc), lambda i, rb, e, j, ok: (rb[i], j[i])),
    )
    return pl.pallas_call(
        kern,
        grid_spec=grid_spec,
        out_shape=jax.ShapeDtypeStruct((n_rows, dout), out_dtype),
        compiler_params=_cparams("arbitrary"),
        name=name,
    )(*sched, x, *weights, *biases)


def _moe_combine_kernel(pos_ref, y_hbm, x_ref, w_ref, nw_ref, o_ref, buf, sem, *, tt):
    i = pl.program_id(0)
    n = pl.num_programs(0)

    def start_block(blk, slot):
        def issue(r, carry):
            for kk in range(TOP_K):
                _row_copy(y_hbm, buf.at[slot, kk], sem.at[slot], pos_ref[(blk * tt + r) * TOP_K + kk], r).start()
            return carry
        lax.fori_loop(0, tt, issue, 0)

    @pl.when(i == 0)
    def _():
        start_block(0, 0)

    @pl.when(i + 1 < n)
    def _():
        start_block(i + 1, (i + 1) % 2)

    slot = i % 2

    def wait(r, carry):
        for kk in range(TOP_K):
            _row_copy(y_hbm, buf.at[slot, kk], sem.at[slot], 0, r).wait()
        return carry
    lax.fori_loop(0, tt, wait, 0)
    acc = x_ref[...]
    w = w_ref[...]
    for kk in range(TOP_K):
        acc = acc + w[:, kk:kk + 1] * buf[slot, kk]
    o_ref[...] = acc * lax.rsqrt(jnp.mean(acc * acc, axis=-1, keepdims=True) + RMS_EPS) * nw_ref[...]


def moe_combine(pos, y_rows, x, top_w, norm_w, *, tt):
    t, d = x.shape
    grid_spec = pltpu.PrefetchScalarGridSpec(
        num_scalar_prefetch=1,
        grid=(t // tt,),
        in_specs=[pl.BlockSpec(memory_space=pl.ANY),
                  pl.BlockSpec((tt, d), lambda i, pos: (i, 0)),
                  pl.BlockSpec((tt, 128), lambda i, pos: (i, 0)),
                  pl.BlockSpec((1, d), lambda i, pos: (0, 0))],
        out_specs=pl.BlockSpec((tt, d), lambda i, pos: (i, 0)),
        scratch_shapes=[pltpu.VMEM((2, TOP_K, tt, d), F32), pltpu.SemaphoreType.DMA((2,))],
    )
    return pl.pallas_call(
        functools.partial(_moe_combine_kernel, tt=tt),
        grid_spec=grid_spec,
        out_shape=jax.ShapeDtypeStruct((t, d), F32),
        compiler_params=_cparams("arbitrary"),
        name="moe_combine",
    )(pos, y_rows, x, top_w, norm_w.reshape(1, d))


def _moe_schedule(top_idx, n_tok):
    tm = MOE_ROWS
    n_asg = n_tok * TOP_K
    n_blocks = -(-n_asg // tm) + N_EXP
    n_rows = n_blocks * tm
    exp_flat = top_idx.reshape(-1)
    order = jnp.argsort(exp_flat).astype(I32)
    exp_sorted = exp_flat[order]
    counts = jnp.bincount(exp_flat, length=N_EXP).astype(I32)
    start = jnp.cumsum(counts) - counts
    padded = (counts + tm - 1) // tm * tm
    pend = jnp.cumsum(padded)
    pstart = pend - padded
    dest = (pstart[exp_sorted] + (jnp.arange(n_asg, dtype=I32) - start[exp_sorted])).astype(I32)
    row_tok = jnp.zeros((n_rows,), I32).at[dest].set(order // TOP_K)
    pos = jnp.zeros((n_asg,), I32).at[order].set(dest)
    blk = jnp.arange(n_blocks, dtype=I32)
    block_exp = jnp.minimum(jnp.searchsorted(pend, blk * tm, side="right"), N_EXP - 1).astype(I32)
    blk_ok = (blk * tm < pend[-1]).astype(I32)
    first_blk = pstart // tm
    num_blk = (padded // tm).at[N_EXP - 1].add(n_blocks - pend[-1] // tm)
    return row_tok, pos, block_exp, blk_ok, first_blk.astype(I32), num_blk.astype(I32), n_blocks, n_rows


def _moe_items(block_exp, blk_ok, first_blk, num_blk, n_blocks, n_j):
    it = jnp.arange(n_blocks * n_j, dtype=I32)
    e = block_exp[it // n_j]
    fb = first_blk[e]
    nb = jnp.maximum(num_blk[e], 1)
    local = it - n_j * fb
    j = local // nb
    rb = fb + local % nb
    return rb.astype(I32), e.astype(I32), j.astype(I32), blk_ok[rb]


def kernel(x_prompt, x_sample, cache_fox_k, cache_fox_v, cache_fox_logf, state_dn, state_dn_conv, page_table,
           attn_norm_w, w_in, dn_conv_w, dn_a_log, dn_dt_bias, fox_f_bias, dn_norm_w, fox_norm_w, w_out,
           ffn_norm_w, w_router, b_router, w_gate, b_gate, w_up, b_up, w_down, b_down, final_norm_w):
    bp, lp, d = x_prompt.shape
    bs, ts, _ = x_sample.shape
    tp, tsm = bp * lp, bs * ts
    t = tp + tsm
    hd, hf, dh = N_DN_HEADS, N_FOX_HEADS, HEAD_DIM
    qk_dim = hd * dh
    conv_dim = 3 * qk_dim
    fox_dim = hf * dh
    n_pages = page_table.shape[1]
    page = cache_fox_k.shape[2]
    lyr = 0

    x = jnp.concatenate([x_prompt.reshape(tp, d), x_sample.reshape(tsm, d)], axis=0)

    o_z = conv_dim
    o_b = o_z + qk_dim
    o_a = o_b + hd
    o_fq = o_a + hd
    o_ff = o_fq + 3 * fox_dim
    w = w_in[lyr]
    w_big = jnp.concatenate([w[:, :o_b], w[:, o_fq:o_ff]], axis=1).astype(BF16)
    w_small = jnp.concatenate([w[:, o_b:o_fq], w[:, o_ff:], jnp.zeros((d, 128 - 2 * hd - hf), F32)], axis=1).astype(BF16)
    h = rmsnorm_rows(x, attn_norm_w[lyr], BF16, tr=320)
    proj = matmul(h, w_big, tm=640, tn=1024, name="in_proj")
    small = matmul(h, w_small, tm=640, tn=128, name="in_proj_gates")
    cb_q, cb_k, cb_v, cb_z = 0, hd, 2 * hd, 3 * hd
    cb_fq = 4 * hd
    cb_fk = cb_fq + hf
    cb_fv = cb_fk + hf

    gparams = jnp.zeros((8, 128), F32)
    gparams = gparams.at[0, LANE_G:LANE_G + hd].set(dn_a_log[lyr])
    gparams = gparams.at[1, LANE_G:LANE_G + hd].set(dn_dt_bias[lyr])
    gparams = gparams.at[2, LANE_LOGF:LANE_LOGF + hf].set(fox_f_bias[lyr])
    chunk_p = math.gcd(lp, DN_CHUNK)
    g1p, g2p = gates(small[:tp], gparams, rows_blk=lp, seg_g=chunk_p, seg_f=lp)
    g1s, g2s = gates(small[tp:], gparams, rows_blk=tsm, seg_g=ts, seg_f=ts)

    nc_p = lp // chunk_p
    gr_p = g2p[:, LANE_G:LANE_G + hd].reshape(bp, nc_p, chunk_p, hd).transpose(0, 3, 1, 2).reshape(bp, hd, nc_p, 1, chunk_p)
    conv_w = dn_conv_w[lyr]
    dnw = dn_norm_w[lyr].reshape(1, dh)
    o_dn_p, s_fin_p = gdn_heads(proj, proj, proj, proj, (cb_q, cb_k, cb_v, cb_z), conv_w, g1p, g2p, gr_p, dnw, None,
                                n_seq=bp, chunk=chunk_p, n_chunks=nc_p, pad_rows=0)

    frame = 8
    lead = frame - ts
    proj_s = proj[tp:].reshape(bs, ts, -1)
    hist = jnp.concatenate([jnp.zeros((bs, lead - (CONV_W - 1), conv_dim), F32), state_dn_conv[lyr]], axis=1)
    xpad = jnp.concatenate([hist, proj_s[:, :, :conv_dim]], axis=1).reshape(bs * frame, conv_dim)
    z_s = jnp.pad(proj_s[:, :, conv_dim:conv_dim + qk_dim], ((0, 0), (lead, 0), (0, 0))).reshape(bs * frame, qk_dim)
    g1s8 = jnp.pad(g1s.reshape(bs, ts, 128), ((0, 0), (lead, 0), (0, 0))).reshape(bs * frame, 128)
    g2s8 = jnp.pad(g2s.reshape(bs, ts, 128), ((0, 0), (lead, 0), (0, 0))).reshape(bs * frame, 128)
    gr_s = g2s8[:, LANE_G:LANE_G + hd].reshape(bs, frame, hd).transpose(0, 2, 1).reshape(bs, hd, 1, 1, frame)
    o_dn_s8, s_fin_s = gdn_heads(xpad, xpad, xpad, z_s, (cb_q, cb_k, cb_v, 0), conv_w, g1s8, g2s8, gr_s, dnw,
                                 state_dn[lyr], n_seq=bs, chunk=frame, n_chunks=1, pad_rows=lead)
    o_dn_s = o_dn_s8.reshape(bs, frame, qk_dim)[:, lead:].reshape(tsm, qk_dim)

    tq = tk = 256
    fnw = fox_norm_w[lyr].reshape(1, dh)
    ck_p = g2p[:, LANE_LOGF:LANE_LOGF + hf].reshape(bp, lp // tk, tk, hf).transpose(0, 3, 1, 2).reshape(bp, hf, lp // tk, 1, tk)
    o_fx_p = fox_prompt(proj, (cb_fq, cb_fk, cb_fv), g2p, ck_p, fnw, n_seq=bp, seq=lp, tq=tq, tk=tk)

    pt_flat = page_table.reshape(-1).astype(I32)
    cpast = fox_cpast(pt_flat, cache_fox_logf, n_seq=bs, n_pages=n_pages)
    cpg = cpast.reshape(bs * n_pages, 1, page * hf)
    fq_s = proj_s[:, :, cb_fq * dh:cb_fq * dh + fox_dim].reshape(bs, ts * hf, dh)
    fk_s = proj_s[:, :, cb_fk * dh:cb_fk * dh + fox_dim].reshape(bs, ts * hf, dh)
    fv_s = proj_s[:, :, cb_fv * dh:cb_fv * dh + fox_dim].reshape(bs, ts * hf, dh)
    c_new = g2s[:, LANE_LOGF:LANE_LOGF + hf].reshape(bs, ts * hf)
    o_fx_s = fox_sample(pt_flat, fq_s, cache_fox_k, cache_fox_v, cpg, c_new.reshape(bs, ts * hf, 1), fk_s, fv_s,
                        c_new.reshape(bs, 1, ts * hf), fnw, n_seq=bs, n_pages=n_pages, n_new=ts)

    cat = jnp.concatenate([jnp.concatenate([o_dn_p, o_fx_p], axis=1),
                           jnp.concatenate([o_dn_s, o_fx_s.reshape(tsm, fox_dim)], axis=1)], axis=0).astype(BF16)
    x2 = matmul(cat, w_out[lyr].astype(BF16), x, tm=640, tn=1024, name="out_proj")

    top_idx, top_w = router(x2, ffn_norm_w[lyr], w_router[lyr], b_router[lyr], tr=320)
    row_tok, pos, block_exp, blk_ok, first_blk, num_blk, n_blocks, n_rows = _moe_schedule(top_idx[:, :TOP_K], t)
    sched = _moe_items(block_exp, blk_ok, first_blk, num_blk, n_blocks, d // MOE_COLS)
    xs = moe_gather(row_tok, x2, ffn_norm_w[lyr], tg=128)
    hact = moe_grouped(sched, xs, (w_gate, w_up), (b_gate[lyr].reshape(N_EXP, 1, -1), b_up[lyr].reshape(N_EXP, 1, -1)),
                       out_dtype=BF16, kern=_moe_up_kernel, name="moe_up")
    y_rows = moe_grouped(sched, hact, (w_down,), (b_down[lyr].reshape(N_EXP, 1, -1),),
                         out_dtype=F32, kern=_moe_down_kernel, name="moe_down")
    y = moe_combine(pos, y_rows, x2, top_w, final_norm_w, tt=64)

    npg = lp // page
    y_prompt = y[:tp].reshape(bp, lp, d)
    y_sample = y[tp:].reshape(bs, ts, d)
    fk_p = proj[:tp, cb_fk * dh:cb_fk * dh + fox_dim].reshape(1, bp, npg, page, hf, dh)
    fv_p = proj[:tp, cb_fv * dh:cb_fv * dh + fox_dim].reshape(1, bp, npg, page, hf, dh)
    lf_p = g1p[:, LANE_LOGF:LANE_LOGF + hf].reshape(1, bp, npg, page, hf)
    conv_p = proj[:tp, :conv_dim].reshape(bp, lp, conv_dim)[:, lp - (CONV_W - 1):][None]
    lf_s = g1s[:, LANE_LOGF:LANE_LOGF + hf].reshape(1, bs, ts, hf)
    conv_s = xpad.reshape(bs, frame, conv_dim)[:, frame - (CONV_W - 1):][None]
    return (y_prompt, y_sample, fk_p, fv_p, lf_p, s_fin_p[None], conv_p,
            fk_s.reshape(1, bs, ts, hf, dh), fv_s.reshape(1, bs, ts, hf, dh), lf_s, s_fin_s[None], conv_s)
```

```python
import functools
import math

import jax
import jax.numpy as jnp
from jax import lax
from jax.experimental import pallas as pl
from jax.experimental.pallas import tpu as pltpu

F32 = jnp.float32
BF16 = jnp.bfloat16
I32 = jnp.int32

HEAD_DIM = 128
N_DN_HEADS = 16
N_FOX_HEADS = 16
CONV_W = 4
DN_CHUNK = 64
N_EXP = 32
TOP_K = 4
SWIGLU_LIMIT = 7.0
SWIGLU_ALPHA = 1.702
RMS_EPS = 1e-6
L2_EPS = 1e-6

LANE_BETA = 0
LANE_G = 16
LANE_LOGF = 32

VMEM_LIMIT = 56 * 1024 * 1024
MOE_ROWS = 256
MOE_COLS = 512
NEG_INF = float("-inf")


def _cparams(*sem):
    return pltpu.CompilerParams(dimension_semantics=sem, vmem_limit_bytes=VMEM_LIMIT)


def _rmsnorm_kernel(x_ref, w_ref, o_ref):
    x = x_ref[...]
    y = x * lax.rsqrt(jnp.mean(x * x, axis=-1, keepdims=True) + RMS_EPS)
    o_ref[...] = (y * w_ref[...]).astype(o_ref.dtype)


def rmsnorm_rows(x, w, out_dtype, tr):
    t, d = x.shape
    return pl.pallas_call(
        _rmsnorm_kernel,
        grid=(t // tr,),
        in_specs=[pl.BlockSpec((tr, d), lambda i: (i, 0)), pl.BlockSpec((1, d), lambda i: (0, 0))],
        out_specs=pl.BlockSpec((tr, d), lambda i: (i, 0)),
        out_shape=jax.ShapeDtypeStruct((t, d), out_dtype),
        compiler_params=_cparams("parallel"),
        name="rmsnorm",
    )(x, w.reshape(1, d))


def _mm_kernel(a_ref, b_ref, *rest, has_res):
    o_ref = rest[-1]
    acc = jnp.dot(a_ref[...], b_ref[...], preferred_element_type=F32)
    if has_res:
        acc = acc + rest[0][...]
    o_ref[...] = acc


def matmul(a, b, res=None, *, tm, tn, name):
    m, k = a.shape
    n = b.shape[1]
    in_specs = [pl.BlockSpec((tm, k), lambda j, i: (i, 0)), pl.BlockSpec((k, tn), lambda j, i: (0, j))]
    args = [a, b]
    if res is not None:
        in_specs.append(pl.BlockSpec((tm, tn), lambda j, i: (i, j)))
        args.append(res)
    return pl.pallas_call(
        functools.partial(_mm_kernel, has_res=res is not None),
        grid=(n // tn, m // tm),
        in_specs=in_specs,
        out_specs=pl.BlockSpec((tm, tn), lambda j, i: (i, j)),
        out_shape=jax.ShapeDtypeStruct((m, n), F32),
        compiler_params=_cparams("parallel", "parallel"),
        name=name,
    )(*args)


def _softplus(x):
    return jnp.maximum(x, 0.0) + jnp.log1p(jnp.exp(-jnp.abs(x)))


def _gates_kernel(s_ref, p_ref, g1_ref, g2_ref, *, seg_g, seg_f):
    x = s_ref[...]
    lane = lax.broadcasted_iota(I32, x.shape, 1)
    row = lax.broadcasted_iota(I32, x.shape, 0)
    is_b = lane < LANE_G
    is_g = jnp.logical_and(lane >= LANE_G, lane < LANE_LOGF)
    is_f = jnp.logical_and(lane >= LANE_LOGF, lane < LANE_LOGF + N_FOX_HEADS)
    beta = jax.nn.sigmoid(x)
    g = -jnp.exp(p_ref[0:1, :]) * _softplus(x + p_ref[1:2, :])
    lf = -_softplus(-(x + p_ref[2:3, :]))
    g1 = jnp.where(is_b, beta, jnp.where(is_g, g, jnp.where(is_f, lf, 0.0)))
    g1_ref[...] = g1
    c = jnp.where(is_b, 0.0, g1)
    pos = jnp.where(is_g, row & (seg_g - 1), row & (seg_f - 1))
    s = 1
    while s < max(seg_g, seg_f):
        c = c + jnp.where(pos >= s, pltpu.roll(c, s, axis=0), 0.0)
        s *= 2
    g2_ref[...] = c


def gates(small, params, *, rows_blk, seg_g, seg_f):
    t = small.shape[0]
    spec = pl.BlockSpec((rows_blk, 128), lambda i: (i, 0))
    return pl.pallas_call(
        functools.partial(_gates_kernel, seg_g=seg_g, seg_f=seg_f),
        grid=(t // rows_blk,),
        in_specs=[spec, pl.BlockSpec((8, 128), lambda i: (0, 0))],
        out_specs=[spec, spec],
        out_shape=[jax.ShapeDtypeStruct((t, 128), F32)] * 2,
        compiler_params=_cparams("parallel"),
        name="gates",
    )(small, params)


def _bmm(a, b):
    return lax.dot_general(a, b, (((2,), (1,)), ((0,), (0,))), preferred_element_type=F32)


def _bmm_nt(a, b):
    return lax.dot_general(a, b, (((2,), (2,)), ((0,), (0,))), preferred_element_type=F32)


def _split_bf16(a):
    hi = a.astype(BF16)
    return hi, (a - hi.astype(F32)).astype(BF16)


def _bmm_split(a, b):
    return _bmm(a[0], b[0]) + (_bmm(a[0], b[1]) + _bmm(a[1], b[0]))


def _gdn_kernel(q_ref, k_ref, v_ref, z_ref, cwq_ref, cwk_ref, cwv_ref, g1_ref, g2_ref, gr_ref, nw_ref, *rest,
                chunk, n_chunks, pad_rows, has_s0, heads):
    if has_s0:
        s0_ref, o_ref, sf_ref, u_s, w_s, qs_s, kd_s, qk_s, el_s, o_s = rest
    else:
        o_ref, sf_ref, u_s, w_s, qs_s, kd_s, qk_s, el_s, o_s = rest
    c, n = chunk, n_chunks
    rows = c * n
    dk = HEAD_DIM
    h_first = pl.program_id(1) * heads
    row = lax.broadcasted_iota(I32, (rows, dk), 0)
    lane = lax.broadcasted_iota(I32, (rows, dk), 1)
    ii = lax.broadcasted_iota(I32, (c, c), 0)
    jj = lax.broadcasted_iota(I32, (c, c), 1)
    incl = (ii >= jj)[None]
    strict = (ii > jj)[None]
    eye = jnp.where((ii == jj)[None], 1.0, 0.0)
    scale = dk ** -0.5

    def conv_silu(x_ref, cw_ref, cols):
        x = x_ref[:, cols]
        acc = x * cw_ref[CONV_W - 1:CONV_W, cols]
        for j in range(CONV_W - 1):
            sh = CONV_W - 1 - j
            acc = acc + jnp.where(row >= sh, pltpu.roll(x, sh, axis=0), 0.0) * cw_ref[j:j + 1, cols]
        y = acc * jax.nn.sigmoid(acc)
        if pad_rows:
            y = jnp.where(row >= pad_rows, y, 0.0)
        return y

    def l2n(x):
        return x * lax.rsqrt(jnp.sum(x * x, axis=-1, keepdims=True) + L2_EPS)

    for hh in range(heads):
        cols = slice(hh * dk, (hh + 1) * dk)
        q = l2n(conv_silu(q_ref, cwq_ref, cols))
        k = l2n(conv_silu(k_ref, cwk_ref, cols))
        v = conv_silu(v_ref, cwv_ref, cols)
        beta = jnp.sum(jnp.where(lane == LANE_BETA + h_first + hh, g1_ref[...], 0.0), axis=1, keepdims=True)
        gc = jnp.sum(jnp.where(lane == LANE_G + h_first + hh, g2_ref[...], 0.0), axis=1, keepdims=True)
        q3 = q.reshape(n, c, dk)
        k3 = k.reshape(n, c, dk)
        v3 = v.reshape(n, c, dk)
        beta3 = beta.reshape(n, c, 1)
        gc3 = gc.reshape(n, c, 1)
        gr3 = gr_ref[hh]
        decay = jnp.where(incl, jnp.exp(jnp.where(incl, gc3 - gr3, 0.0)), 0.0)
        kb = k3 * beta3
        k3b = k3.astype(BF16)
        a = jnp.where(strict, _bmm_nt(kb.astype(BF16), k3b) * decay, 0.0)
        tinv = eye - a
        xs = _split_bf16(a)
        p = 2
        while p < c:
            xs = _split_bf16(_bmm_split(xs, xs))
            tinv = tinv + _bmm_split(_split_bf16(tinv), xs)
            p *= 2
        tb = tinv.astype(BF16)
        u_s[hh] = _bmm(tb, (v3 * beta3).astype(BF16))
        w_s[hh] = _bmm(tb, (kb * jnp.exp(gc3)).astype(BF16))
        qsc = q3 * scale
        qk_s[hh] = _bmm_nt(qsc.astype(BF16), k3b) * decay
        qs_s[hh] = qsc * jnp.exp(gc3)
        gl3 = gc3[:, c - 1:c, :]
        kd_s[hh] = k3 * jnp.exp(gl3 - gc3)
        el_s[hh] = jnp.broadcast_to(jnp.exp(gl3), (n, 1, dk))

    def step(i, states):
        new_states = []
        for hh in range(heads):
            s = states[hh]
            sb = s.astype(BF16)
            v_new = u_s[hh, i] - jnp.dot(w_s[hh, i].astype(BF16), sb, preferred_element_type=F32)
            vb = v_new.astype(BF16)
            o_s[hh, i] = (jnp.dot(qs_s[hh, i].astype(BF16), sb, preferred_element_type=F32)
                          + jnp.dot(qk_s[hh, i].astype(BF16), vb, preferred_element_type=F32))
            new_states.append(s * el_s[hh, i] + lax.dot_general(kd_s[hh, i].astype(BF16), vb, (((0,), (0,)), ((), ())),
                                                                preferred_element_type=F32))
        return tuple(new_states)

    init = tuple(s0_ref[hh] if has_s0 else jnp.zeros((dk, dk), F32) for hh in range(heads))
    finals = step(0, init) if n == 1 else lax.fori_loop(0, n, step, init)
    for hh in range(heads):
        cols = slice(hh * dk, (hh + 1) * dk)
        sf_ref[hh] = finals[hh]
        o = o_s[hh].reshape(rows, dk)
        on = o * lax.rsqrt(jnp.mean(o * o, axis=-1, keepdims=True) + RMS_EPS) * nw_ref[...]
        z = z_ref[:, cols]
        o_ref[:, cols] = on * (z * jax.nn.sigmoid(z))


def gdn_heads(xq, xk, xv, xz, col_off, conv_w, g1, g2, gr, norm_w, s0, *, n_seq, chunk, n_chunks, pad_rows, heads):
    rows = chunk * n_chunks
    n_h = N_DN_HEADS
    wide = heads * HEAD_DIM
    assert n_h % heads == 0 and all(off % heads == 0 for off in col_off)
    qo, ko, vo, zo = (off // heads for off in col_off)

    def xspec(off):
        return pl.BlockSpec((rows, wide), lambda b, h, off=off: (b, off + h))

    def cwspec(off):
        return pl.BlockSpec((CONV_W, wide), lambda b, h, off=off: (0, off + h))

    gspec = pl.BlockSpec((rows, 128), lambda b, h: (b, 0))
    sspec = pl.BlockSpec((None, heads, HEAD_DIM, HEAD_DIM), lambda b, h: (b, h, 0, 0))
    in_specs = [xspec(qo), xspec(ko), xspec(vo), xspec(zo),
                cwspec(0), cwspec(n_h // heads), cwspec(2 * n_h // heads), gspec, gspec,
                pl.BlockSpec((None, heads, n_chunks, 1, chunk), lambda b, h: (b, h, 0, 0, 0)),
                pl.BlockSpec((1, HEAD_DIM), lambda b, h: (0, 0))]
    args = [xq, xk, xv, xz, conv_w, conv_w, conv_w, g1, g2, gr, norm_w]
    if s0 is not None:
        in_specs.append(sspec)
        args.append(s0)
    big = pltpu.VMEM((heads, n_chunks, chunk, HEAD_DIM), F32)
    return pl.pallas_call(
        functools.partial(_gdn_kernel, chunk=chunk, n_chunks=n_chunks, pad_rows=pad_rows, has_s0=s0 is not None,
                          heads=heads),
        grid=(n_seq, n_h // heads),
        in_specs=in_specs,
        out_specs=[pl.BlockSpec((rows, wide), lambda b, h: (b, h)), sspec],
        out_shape=[jax.ShapeDtypeStruct((n_seq * rows, n_h * HEAD_DIM), F32),
                   jax.ShapeDtypeStruct((n_seq, n_h, HEAD_DIM, HEAD_DIM), F32)],
        scratch_shapes=[big, big, big, big, pltpu.VMEM((heads, n_chunks, chunk, chunk), F32),
                        pltpu.VMEM((heads, n_chunks, 1, HEAD_DIM), F32), big],
        compiler_params=_cparams("parallel", "parallel"),
        name="gdn_c%d" % chunk,
    )(*args)


def _fox_prompt_kernel(q_ref, k_ref, v_ref, g2_ref, ck_ref, nw_ref, o_ref, *, tq, tk):
    h = pl.program_id(1)
    i = pl.program_id(2)
    d = HEAD_DIM
    scale = d ** -0.5
    q = q_ref[...].astype(BF16)
    lane = lax.broadcasted_iota(I32, (tq, 128), 1)
    cq = jnp.sum(jnp.where(lane == LANE_LOGF + h, g2_ref[...], 0.0), axis=1, keepdims=True)
    causal = lax.broadcasted_iota(I32, (tq, tk), 0) >= lax.broadcasted_iota(I32, (tq, tk), 1)

    def block(j, carry, on_diagonal):
        m, l, acc = carry
        start = pl.multiple_of(j * tk, tk)
        kj = k_ref[pl.ds(start, tk), :].astype(BF16)
        vj = v_ref[pl.ds(start, tk), :].astype(BF16)
        s = lax.dot_general(q, kj, (((1,), (1,)), ((), ())), preferred_element_type=F32) * scale
        s = s + cq - ck_ref[j]
        if on_diagonal:
            s = jnp.where(causal, s, NEG_INF)
        m_new = jnp.maximum(m, jnp.max(s, axis=1, keepdims=True))
        p = jnp.exp(s - m_new)
        corr = jnp.exp(m - m_new)
        l = l * corr + jnp.sum(p, axis=1, keepdims=True)
        acc = acc * corr + jnp.dot(p.astype(BF16), vj, preferred_element_type=F32)
        return m_new, l, acc

    init = (jnp.full((tq, 1), NEG_INF, F32), jnp.zeros((tq, 1), F32), jnp.zeros((tq, d), F32))
    carry = lax.fori_loop(0, i, lambda j, c: block(j, c, False), init)
    _, l, acc = block(i, carry, True)
    o = acc / l
    o_ref[...] = o * lax.rsqrt(jnp.mean(o * o, axis=-1, keepdims=True) + RMS_EPS) * nw_ref[...]


def fox_prompt(proj, col_off, g2, ck, norm_w, *, n_seq, seq, tq, tk):
    assert tq == tk
    qo, ko, vo = col_off
    hh = N_FOX_HEADS
    nq = seq // tq
    return pl.pallas_call(
        functools.partial(_fox_prompt_kernel, tq=tq, tk=tk),
        grid=(n_seq, hh, nq),
        in_specs=[pl.BlockSpec((tq, HEAD_DIM), lambda b, h, i: (b * nq + i, qo + h)),
                  pl.BlockSpec((seq, HEAD_DIM), lambda b, h, i: (b, ko + h)),
                  pl.BlockSpec((seq, HEAD_DIM), lambda b, h, i: (b, vo + h)),
                  pl.BlockSpec((tq, 128), lambda b, h, i: (b * nq + i, 0)),
                  pl.BlockSpec((None, None, seq // tk, 1, tk), lambda b, h, i: (b, h, 0, 0, 0)),
                  pl.BlockSpec((1, HEAD_DIM), lambda b, h, i: (0, 0))],
        out_specs=pl.BlockSpec((tq, HEAD_DIM), lambda b, h, i: (b * nq + i, h)),
        out_shape=jax.ShapeDtypeStruct((n_seq * seq, hh * HEAD_DIM), F32),
        compiler_params=_cparams("parallel", "parallel", "arbitrary"),
        name="fox_prompt",
    )(proj, proj, proj, g2, ck, norm_w)


def _fox_cpast_kernel(pt_ref, *rest, pages_per_step):
    del pt_ref
    lf_refs = rest[:pages_per_step]
    o_ref, carry = rest[pages_per_step:]
    p = pl.program_id(1)

    @pl.when(p == 0)
    def _():
        carry[...] = jnp.zeros_like(carry)

    ps = lf_refs[0].shape[0]
    ii = lax.broadcasted_iota(I32, (ps, ps), 0)
    jj = lax.broadcasted_iota(I32, (ps, ps), 1)
    later = jnp.where(jj > ii, 1.0, 0.0)
    run = carry[...]
    for i in range(pages_per_step):
        lf = lf_refs[i][...]
        excl = jnp.dot(later, lf, precision=lax.Precision.HIGHEST, preferred_element_type=F32)
        o_ref[pages_per_step - 1 - i] = -(excl + run)
        run = run + jnp.sum(lf, axis=0, keepdims=True)
    carry[...] = run


def fox_cpast(page_table_flat, cache_logf, *, n_seq, n_pages, pages_per_step):
    _, _, ps, hh = cache_logf.shape
    g = pages_per_step
    n_steps = n_pages // g

    def page_spec(i):
        return pl.BlockSpec((None, None, ps, hh),
                            lambda b, p, pt, i=i: (0, pt[b * n_pages + n_pages - 1 - p * g - i], 0, 0))

    grid_spec = pltpu.PrefetchScalarGridSpec(
        num_scalar_prefetch=1,
        grid=(n_seq, n_steps),
        in_specs=[page_spec(i) for i in range(g)],
        out_specs=pl.BlockSpec((None, g, ps, hh), lambda b, p, pt: (b, n_steps - 1 - p, 0, 0)),
        scratch_shapes=[pltpu.VMEM((1, hh), F32)],
    )
    return pl.pallas_call(
        functools.partial(_fox_cpast_kernel, pages_per_step=g),
        grid_spec=grid_spec,
        out_shape=jax.ShapeDtypeStruct((n_seq, n_pages, ps, hh), F32),
        compiler_params=_cparams("parallel", "arbitrary"),
        name="fox_cpast",
    )(page_table_flat, *([cache_logf] * g))


def _fox_sample_kernel(pt_ref, q_ref, *rest, n_steps, n_new, pages_per_step):
    del pt_ref
    g = pages_per_step
    k_refs = rest[:g]
    v_refs = rest[g:2 * g]
    cpg_ref, cq_ref, kn_ref, vn_ref, cn_ref, nw_ref, o_ref, m_s, l_s, acc_s = rest[2 * g:]
    p = pl.program_id(1)
    d = HEAD_DIM
    hh = N_FOX_HEADS
    nr = n_new * hh

    @pl.when(p == 0)
    def _():
        m_s[...] = jnp.full_like(m_s, NEG_INF)
        l_s[...] = jnp.zeros_like(l_s)
        acc_s[...] = jnp.zeros_like(acc_s)

    q = (q_ref[...] * (d ** -0.5)).astype(BF16)
    cq = cq_ref[...]

    def update(scores, vals):
        m = m_s[...]
        m_new = m
        for s in scores:
            m_new = jnp.maximum(m_new, jnp.max(s, axis=1, keepdims=True))
        corr = jnp.exp(m - m_new)
        l = l_s[...] * corr
        acc = acc_s[...] * corr
        for s, val in zip(scores, vals):
            pr = jnp.exp(s - m_new)
            l = l + jnp.sum(pr, axis=1, keepdims=True)
            acc = acc + jnp.dot(pr.astype(BF16), val, preferred_element_type=F32)
        l_s[...] = l
        acc_s[...] = acc
        m_s[...] = m_new

    ps = k_refs[0].shape[0]
    nc = ps * hh
    same_head = ((lax.broadcasted_iota(I32, (nr, nc), 0) & (hh - 1))
                 == (lax.broadcasted_iota(I32, (nr, nc), 1) & (hh - 1)))
    scores, vals = [], []
    for i in range(g):
        kf = k_refs[i][...].reshape(nc, d).astype(BF16)
        s = lax.dot_general(q, kf, (((1,), (1,)), ((), ())), preferred_element_type=F32)
        s = s + cq - cpg_ref[:, i * nc:(i + 1) * nc]
        scores.append(jnp.where(same_head, s, NEG_INF))
        vals.append(v_refs[i][...].reshape(nc, d).astype(BF16))
    update(scores, vals)

    @pl.when(p == n_steps - 1)
    def _():
        kn = kn_ref[...].astype(BF16)
        vn = vn_ref[...].astype(BF16)
        s2 = lax.dot_general(q, kn, (((1,), (1,)), ((), ())), preferred_element_type=F32)
        s2 = s2 + cq - cn_ref[...]
        r = lax.broadcasted_iota(I32, (nr, nr), 0)
        c = lax.broadcasted_iota(I32, (nr, nr), 1)
        keep = jnp.logical_and((r & (hh - 1)) == (c & (hh - 1)), r >= c)
        update([jnp.where(keep, s2, NEG_INF)], [vn])
        o = acc_s[...] / l_s[...]
        o_ref[...] = o * lax.rsqrt(jnp.mean(o * o, axis=-1, keepdims=True) + RMS_EPS) * nw_ref[...]


def fox_sample(page_table_flat, q, cache_k, cache_v, cpg, cq, k_new, v_new, c_new, norm_w, *, n_seq, n_pages, n_new,
               pages_per_step):
    _, _, ps, hh, d = cache_k.shape
    nr = n_new * hh
    g = pages_per_step
    n_steps = n_pages // g

    def cache_spec(i):
        return pl.BlockSpec((None, None, ps, hh, d), lambda b, p, pt, i=i: (0, pt[b * n_pages + p * g + i], 0, 0, 0))

    seq_spec = pl.BlockSpec((None, nr, d), lambda b, p, pt: (b, 0, 0))
    grid_spec = pltpu.PrefetchScalarGridSpec(
        num_scalar_prefetch=1,
        grid=(n_seq, n_steps),
        in_specs=[seq_spec] + [cache_spec(i) for i in range(g)] * 2
        + [pl.BlockSpec((None, 1, g * ps * hh), lambda b, p, pt: (b * n_steps + p, 0, 0)),
           pl.BlockSpec((None, nr, 1), lambda b, p, pt: (b, 0, 0)),
           seq_spec, seq_spec,
           pl.BlockSpec((None, 1, nr), lambda b, p, pt: (b, 0, 0)),
           pl.BlockSpec((1, d), lambda b, p, pt: (0, 0))],
        out_specs=seq_spec,
        scratch_shapes=[pltpu.VMEM((nr, 1), F32), pltpu.VMEM((nr, 1), F32), pltpu.VMEM((nr, d), F32)],
    )
    return pl.pallas_call(
        functools.partial(_fox_sample_kernel, n_steps=n_steps, n_new=n_new, pages_per_step=g),
        grid_spec=grid_spec,
        out_shape=jax.ShapeDtypeStruct((n_seq, nr, d), F32),
        compiler_params=_cparams("parallel", "arbitrary"),
        name="fox_sample",
    )(page_table_flat, q, *([cache_k] * g), *([cache_v] * g), cpg, cq, k_new, v_new, c_new, norm_w)


def _router_kernel(x_ref, nw_ref, wr_ref, br_ref, idx_ref, w_ref, rank_ref, cnt_ref, hp_ref, run):
    i = pl.program_id(0)

    @pl.when(i == 0)
    def _():
        run[...] = jnp.zeros_like(run)

    x = x_ref[...]
    hcur = x * lax.rsqrt(jnp.mean(x * x, axis=-1, keepdims=True) + RMS_EPS) * nw_ref[...]
    logits = jnp.dot(hcur, wr_ref[...], precision=lax.Precision.HIGHEST, preferred_element_type=F32) + br_ref[...]
    tr, ne = logits.shape
    lane = lax.broadcasted_iota(I32, (tr, ne), 1)
    lane_f = lane.astype(F32)
    olane = lax.broadcasted_iota(I32, (tr, 128), 1)
    idx_out = jnp.zeros((tr, 128), I32)
    w_out = jnp.zeros((tr, 128), F32)
    cur = logits
    tops, hots = [], []
    for kk in range(TOP_K):
        mx = jnp.max(cur, axis=1, keepdims=True)
        sel = jnp.min(jnp.where(cur == mx, lane_f, float(ne)), axis=1, keepdims=True).astype(I32)
        tops.append(mx)
        hots.append(lane == sel)
        idx_out = jnp.where(olane == kk, sel, idx_out)
        cur = jnp.where(lane == sel, NEG_INF, cur)
    es = [jnp.exp(t - tops[0]) for t in tops]
    tot = es[0]
    for e in es[1:]:
        tot = tot + e
    for kk in range(TOP_K):
        w_out = jnp.where(olane == kk, es[kk] / tot, w_out)
    idx_ref[...] = idx_out
    w_ref[...] = w_out

    chosen = jnp.zeros((tr, ne), F32)
    for hot in hots:
        chosen = jnp.where(hot, 1.0, chosen)
    before = jnp.where(lax.broadcasted_iota(I32, (tr, tr), 0) > lax.broadcasted_iota(I32, (tr, tr), 1), 1.0, 0.0)
    ahead = jnp.dot(before.astype(BF16), chosen.astype(BF16), preferred_element_type=F32) + run[...]
    rank_out = jnp.zeros((tr, 128), I32)
    for kk in range(TOP_K):
        r = jnp.sum(jnp.where(hots[kk], ahead, 0.0), axis=1, keepdims=True).astype(I32)
        rank_out = jnp.where(olane == kk, r, rank_out)
    rank_ref[...] = rank_out
    run[...] = run[...] + jnp.sum(chosen, axis=0, keepdims=True)
    cnt_ref[...] = run[...]

    half = hcur.shape[1] // 2
    lo = pltpu.bitcast(hcur[:, :half].astype(BF16).astype(F32), jnp.uint32)
    hi = pltpu.bitcast(hcur[:, half:].astype(BF16).astype(F32), jnp.uint32)
    hp_ref[...] = (hi & jnp.uint32(0xFFFF0000)) | (lo >> jnp.uint32(16))


def _unpack_bf16_pairs(words):
    lo = pltpu.bitcast(words << jnp.uint32(16), F32).astype(BF16)
    hi = pltpu.bitcast(words & jnp.uint32(0xFFFF0000), F32).astype(BF16)
    return jnp.concatenate([lo, hi], axis=1)


def router(x, norm_w, w_router, b_router, *, tr):
    t, d = x.shape
    ne = w_router.shape[1]
    tile = pl.BlockSpec((tr, 128), lambda i: (i, 0))
    return pl.pallas_call(
        _router_kernel,
        grid=(t // tr,),
        in_specs=[pl.BlockSpec((tr, d), lambda i: (i, 0)), pl.BlockSpec((1, d), lambda i: (0, 0)),
                  pl.BlockSpec((d, ne), lambda i: (0, 0)), pl.BlockSpec((1, ne), lambda i: (0, 0))],
        out_specs=[tile, tile, tile, pl.BlockSpec((1, ne), lambda i: (0, 0)),
                   pl.BlockSpec((tr, d // 2), lambda i: (i, 0))],
        out_shape=[jax.ShapeDtypeStruct((t, 128), I32), jax.ShapeDtypeStruct((t, 128), F32),
                   jax.ShapeDtypeStruct((t, 128), I32), jax.ShapeDtypeStruct((1, ne), F32),
                   jax.ShapeDtypeStruct((t, d // 2), jnp.uint32)],
        scratch_shapes=[pltpu.VMEM((1, ne), F32)],
        compiler_params=_cparams("arbitrary"),
        name="router",
    )(x, norm_w.reshape(1, d), w_router, b_router.reshape(1, ne))


def _row_copy(src_hbm, dst, sem, src_row, dst_row):
    return pltpu.make_async_copy(src_hbm.at[pl.ds(src_row, 1), :], dst.at[pl.ds(dst_row, 1), :], sem)


def _moe_dispatch_kernel(pos_ref, src_hbm, zero_hbm, dst_hbm, sem, *, tt):
    del zero_hbm
    i = pl.program_id(0)
    n = pl.num_programs(0)

    def copy(tok, dst_row, slot):
        return pltpu.make_async_copy(src_hbm.at[pl.ds(tok, 1), :], dst_hbm.at[pl.ds(dst_row, 1), :], sem.at[slot])

    def issue(r, carry):
        tok = i * tt + r
        for kk in range(TOP_K):
            copy(tok, pos_ref[tok * TOP_K + kk], i % 2).start()
        return carry
    lax.fori_loop(0, tt, issue, 0)

    def drain(slot):
        def wait(r, carry):
            for _ in range(TOP_K):
                copy(0, 0, slot).wait()
            return carry
        lax.fori_loop(0, tt, wait, 0)

    @pl.when(i > 0)
    def _():
        drain((i + 1) % 2)

    @pl.when(i == n - 1)
    def _():
        drain(i % 2)


def moe_dispatch(pos, packed, n_rows, *, tt):
    t, dw = packed.shape
    grid_spec = pltpu.PrefetchScalarGridSpec(
        num_scalar_prefetch=1,
        grid=(t // tt,),
        in_specs=[pl.BlockSpec(memory_space=pl.ANY), pl.BlockSpec(memory_space=pl.ANY)],
        out_specs=pl.BlockSpec(memory_space=pl.ANY),
        scratch_shapes=[pltpu.SemaphoreType.DMA((2,))],
    )
    return pl.pallas_call(
        functools.partial(_moe_dispatch_kernel, tt=tt),
        grid_spec=grid_spec,
        out_shape=jax.ShapeDtypeStruct((n_rows, dw), packed.dtype),
        input_output_aliases={2: 0},
        compiler_params=_cparams("arbitrary"),
        name="moe_dispatch",
    )(pos, packed, jnp.zeros((n_rows, dw), packed.dtype))


def _moe_ffn_kernel(fb_ref, nb_ref, x_hbm, *rest, n_w, n_j, packed_rows):
    w_refs = rest[:n_w]
    b_refs = rest[n_w:2 * n_w]
    o_hbm, wc, xbuf, obuf, xsem, osem = rest[2 * n_w:]
    g = pl.program_id(0)
    e = g // n_j
    j = g % n_j
    fb = fb_ref[e]
    nb = nb_ref[e]
    tm = xbuf.shape[1]
    tc = obuf.shape[2]

    def x_copy(blk, slot):
        return pltpu.make_async_copy(x_hbm.at[pl.ds(blk * tm, tm), :], xbuf.at[slot], xsem.at[slot])

    def o_copy(blk, slot):
        return pltpu.make_async_copy(obuf.at[slot], o_hbm.at[pl.ds(blk * tm, tm), pl.ds(j * tc, tc)], osem.at[slot])

    @pl.when(jnp.logical_and(g == 0, nb > 0))
    def _():
        x_copy(fb, 0).start()

    for i in range(n_w):
        wc[i] = w_refs[i][...].astype(BF16)

    def body(r, carry):
        slot = r % 2
        x_copy(fb + r, slot).wait()

        @pl.when(r + 1 < nb)
        def _():
            x_copy(fb + r + 1, 1 - slot).start()

        @pl.when(r >= 2)
        def _():
            o_copy(fb + r - 2, slot).wait()

        x = _unpack_bf16_pairs(xbuf[slot]) if packed_rows else xbuf[slot]
        outs = [jnp.dot(x, wc[i], preferred_element_type=F32) + b_refs[i][...] for i in range(n_w)]
        if n_w == 2:
            gt = jnp.minimum(outs[0], SWIGLU_LIMIT)
            up = jnp.clip(outs[1], -SWIGLU_LIMIT, SWIGLU_LIMIT)
            res = gt * jax.nn.sigmoid(SWIGLU_ALPHA * gt) * (up + 1.0)
        else:
            res = outs[0]
        obuf[slot] = res.astype(obuf.dtype)
        o_copy(fb + r, slot).start()
        return carry
    lax.fori_loop(0, nb, body, 0)

    @pl.when(nb >= 2)
    def _():
        o_copy(fb + nb - 2, nb % 2).wait()

    @pl.when(nb >= 1)
    def _():
        o_copy(fb + nb - 1, (nb - 1) % 2).wait()

    @pl.when(e == pl.num_programs(0) // n_j - 1)
    def _():
        n_tail = o_hbm.shape[0] // tm - (fb + nb)
        obuf[0] = jnp.zeros(obuf.shape[1:], obuf.dtype)

        def fill(r, carry):
            o_copy(fb + nb + r, 0).start()
            return carry
        lax.fori_loop(0, n_tail, fill, 0)

        def fill_done(r, carry):
            o_copy(fb + nb + r, 0).wait()
            return carry
        lax.fori_loop(0, n_tail, fill_done, 0)

    @pl.when(g + 1 < pl.num_programs(0))
    def _():
        e2 = (g + 1) // n_j

        @pl.when(nb_ref[e2] > 0)
        def _():
            x_copy(fb_ref[e2], 0).start()


def moe_ffn(first_blk, num_blk, x, weights, biases, *, out_dtype, packed_rows, name):
    n_rows, dx = x.shape
    n_exp, din, dout = weights[0].shape[1:]
    tm, tc = MOE_ROWS, MOE_COLS
    n_j = dout // tc
    n_w = len(weights)
    wspec = pl.BlockSpec((None, None, din, tc), lambda g, fb, nb: (0, g // n_j, 0, g % n_j))
    bspec = pl.BlockSpec((None, 1, tc), lambda g, fb, nb: (g // n_j, 0, g % n_j))
    grid_spec = pltpu.PrefetchScalarGridSpec(
        num_scalar_prefetch=2,
        grid=(n_exp * n_j,),
        in_specs=[pl.BlockSpec(memory_space=pl.ANY)] + [wspec] * n_w + [bspec] * n_w,
        out_specs=pl.BlockSpec(memory_space=pl.ANY),
        scratch_shapes=[pltpu.VMEM((n_w, din, tc), BF16), pltpu.VMEM((2, tm, dx), x.dtype),
                        pltpu.VMEM((2, tm, tc), out_dtype),
                        pltpu.SemaphoreType.DMA((2,)), pltpu.SemaphoreType.DMA((2,))],
    )
    return pl.pallas_call(
        functools.partial(_moe_ffn_kernel, n_w=n_w, n_j=n_j, packed_rows=packed_rows),
        grid_spec=grid_spec,
        out_shape=jax.ShapeDtypeStruct((n_rows, dout), out_dtype),
        compiler_params=_cparams("arbitrary"),
        name=name,
    )(first_blk, num_blk, x, *weights, *biases)


def _moe_combine_kernel(pos_ref, y_hbm, x_ref, w_ref, nw_ref, o_head_ref, o_tail_ref, buf, sem, *, tt, n_head):
    i = pl.program_id(0)
    n = pl.num_programs(0)

    def start_block(blk, slot):
        def issue(r, carry):
            for kk in range(TOP_K):
                _row_copy(y_hbm, buf.at[slot, kk], sem.at[slot], pos_ref[(blk * tt + r) * TOP_K + kk], r).start()
            return carry
        lax.fori_loop(0, tt, issue, 0)

    @pl.when(i == 0)
    def _():
        start_block(0, 0)

    @pl.when(i + 1 < n)
    def _():
        start_block(i + 1, (i + 1) % 2)

    slot = i % 2

    def wait(r, carry):
        for kk in range(TOP_K):
            _row_copy(y_hbm, buf.at[slot, kk], sem.at[slot], 0, r).wait()
        return carry
    lax.fori_loop(0, tt, wait, 0)
    acc = x_ref[...]
    w = w_ref[...]
    for kk in range(TOP_K):
        acc = acc + w[:, kk:kk + 1] * buf[slot, kk]
    y = acc * lax.rsqrt(jnp.mean(acc * acc, axis=-1, keepdims=True) + RMS_EPS) * nw_ref[...]

    @pl.when(i < n_head)
    def _():
        o_head_ref[...] = y

    @pl.when(i >= n_head)
    def _():
        o_tail_ref[...] = y


def moe_combine(pos, y_rows, x, top_w, norm_w, *, tt, t_head):
    t, d = x.shape
    n_head = t_head // tt
    grid_spec = pltpu.PrefetchScalarGridSpec(
        num_scalar_prefetch=1,
        grid=(t // tt,),
        in_specs=[pl.BlockSpec(memory_space=pl.ANY),
                  pl.BlockSpec((tt, d), lambda i, pos: (i, 0)),
                  pl.BlockSpec((tt, 128), lambda i, pos: (i, 0)),
                  pl.BlockSpec((1, d), lambda i, pos: (0, 0))],
        out_specs=[pl.BlockSpec((tt, d), lambda i, pos: (jnp.minimum(i, n_head - 1), 0)),
                   pl.BlockSpec((tt, d), lambda i, pos: (jnp.maximum(i - n_head, 0), 0))],
        scratch_shapes=[pltpu.VMEM((2, TOP_K, tt, d), F32), pltpu.SemaphoreType.DMA((2,))],
    )
    return pl.pallas_call(
        functools.partial(_moe_combine_kernel, tt=tt, n_head=n_head),
        grid_spec=grid_spec,
        out_shape=[jax.ShapeDtypeStruct((t_head, d), F32), jax.ShapeDtypeStruct((t - t_head, d), F32)],
        compiler_params=_cparams("arbitrary"),
        name="moe_combine",
    )(pos, y_rows, x, top_w, norm_w.reshape(1, d))


def _moe_layout(top_idx, rank, counts, n_tok):
    tm = MOE_ROWS
    n_blocks = -(-(n_tok * TOP_K) // tm) + N_EXP
    padded = (counts + tm - 1) // tm * tm
    pstart = jnp.cumsum(padded) - padded
    pos = (pstart[top_idx] + rank).reshape(-1).astype(I32)
    return pos, (pstart // tm).astype(I32), (padded // tm).astype(I32), n_blocks * tm


def kernel(x_prompt, x_sample, cache_fox_k, cache_fox_v, cache_fox_logf, state_dn, state_dn_conv, page_table,
           attn_norm_w, w_in, dn_conv_w, dn_a_log, dn_dt_bias, fox_f_bias, dn_norm_w, fox_norm_w, w_out,
           ffn_norm_w, w_router, b_router, w_gate, b_gate, w_up, b_up, w_down, b_down, final_norm_w):
    bp, lp, d = x_prompt.shape
    bs, ts, _ = x_sample.shape
    tp, tsm = bp * lp, bs * ts
    t = tp + tsm
    hd, hf, dh = N_DN_HEADS, N_FOX_HEADS, HEAD_DIM
    qk_dim = hd * dh
    conv_dim = 3 * qk_dim
    fox_dim = hf * dh
    n_pages = page_table.shape[1]
    page = cache_fox_k.shape[2]
    lyr = 0

    x = jnp.concatenate([x_prompt.reshape(tp, d), x_sample.reshape(tsm, d)], axis=0)

    o_z = conv_dim
    o_b = o_z + qk_dim
    o_a = o_b + hd
    o_fq = o_a + hd
    o_ff = o_fq + 3 * fox_dim
    w = w_in[lyr]
    w_big = jnp.concatenate([w[:, :o_b], w[:, o_fq:o_ff]], axis=1).astype(BF16)
    w_small = jnp.concatenate([w[:, o_b:o_fq], w[:, o_ff:], jnp.zeros((d, 128 - 2 * hd - hf), F32)], axis=1).astype(BF16)
    h = rmsnorm_rows(x, attn_norm_w[lyr], BF16, tr=320)
    proj = matmul(h, w_big, tm=640, tn=1024, name="in_proj")
    small = matmul(h, w_small, tm=640, tn=128, name="in_proj_gates")
    cb_q, cb_k, cb_v, cb_z = 0, hd, 2 * hd, 3 * hd
    cb_fq = 4 * hd
    cb_fk = cb_fq + hf
    cb_fv = cb_fk + hf

    gparams = jnp.zeros((8, 128), F32)
    gparams = gparams.at[0, LANE_G:LANE_G + hd].set(dn_a_log[lyr])
    gparams = gparams.at[1, LANE_G:LANE_G + hd].set(dn_dt_bias[lyr])
    gparams = gparams.at[2, LANE_LOGF:LANE_LOGF + hf].set(fox_f_bias[lyr])
    chunk_p = math.gcd(lp, DN_CHUNK)
    g1p, g2p = gates(small[:tp], gparams, rows_blk=lp, seg_g=chunk_p, seg_f=lp)
    g1s, g2s = gates(small[tp:], gparams, rows_blk=tsm, seg_g=ts, seg_f=ts)

    nc_p = lp // chunk_p
    gr_p = g2p[:, LANE_G:LANE_G + hd].reshape(bp, nc_p, chunk_p, hd).transpose(0, 3, 1, 2).reshape(bp, hd, nc_p, 1, chunk_p)
    conv_w = dn_conv_w[lyr]
    dnw = dn_norm_w[lyr].reshape(1, dh)
    o_dn_p, s_fin_p = gdn_heads(proj, proj, proj, proj, (cb_q, cb_k, cb_v, cb_z), conv_w, g1p, g2p, gr_p, dnw, None,
                                n_seq=bp, chunk=chunk_p, n_chunks=nc_p, pad_rows=0, heads=2)

    frame = 8
    lead = frame - ts
    proj_s = proj[tp:].reshape(bs, ts, -1)
    hist = jnp.concatenate([jnp.zeros((bs, lead - (CONV_W - 1), conv_dim), F32), state_dn_conv[lyr]], axis=1)
    xpad = jnp.concatenate([hist, proj_s[:, :, :conv_dim]], axis=1).reshape(bs * frame, conv_dim)
    z_s = jnp.pad(proj_s[:, :, conv_dim:conv_dim + qk_dim], ((0, 0), (lead, 0), (0, 0))).reshape(bs * frame, qk_dim)
    g1s8 = jnp.pad(g1s.reshape(bs, ts, 128), ((0, 0), (lead, 0), (0, 0))).reshape(bs * frame, 128)
    g2s8 = jnp.pad(g2s.reshape(bs, ts, 128), ((0, 0), (lead, 0), (0, 0))).reshape(bs * frame, 128)
    gr_s = g2s8[:, LANE_G:LANE_G + hd].reshape(bs, frame, hd).transpose(0, 2, 1).reshape(bs, hd, 1, 1, frame)
    o_dn_s8, s_fin_s = gdn_heads(xpad, xpad, xpad, z_s, (cb_q, cb_k, cb_v, 0), conv_w, g1s8, g2s8, gr_s, dnw,
                                 state_dn[lyr], n_seq=bs, chunk=frame, n_chunks=1, pad_rows=lead, heads=hd)
    o_dn_s = o_dn_s8.reshape(bs, frame, qk_dim)[:, lead:].reshape(tsm, qk_dim)

    tq = tk = 512
    fnw = fox_norm_w[lyr].reshape(1, dh)
    ck_p = g2p[:, LANE_LOGF:LANE_LOGF + hf].reshape(bp, lp // tk, tk, hf).transpose(0, 3, 1, 2).reshape(bp, hf, lp // tk, 1, tk)
    o_fx_p = fox_prompt(proj, (cb_fq, cb_fk, cb_fv), g2p, ck_p, fnw, n_seq=bp, seq=lp, tq=tq, tk=tk)

    pt_flat = page_table.reshape(-1).astype(I32)
    cpast = fox_cpast(pt_flat, cache_fox_logf, n_seq=bs, n_pages=n_pages, pages_per_step=16)
    pg = 4
    cpg = cpast.reshape(bs * n_pages // pg, 1, pg * page * hf)
    fq_s = proj_s[:, :, cb_fq * dh:cb_fq * dh + fox_dim].reshape(bs, ts * hf, dh)
    fk_s = proj_s[:, :, cb_fk * dh:cb_fk * dh + fox_dim].reshape(bs, ts * hf, dh)
    fv_s = proj_s[:, :, cb_fv * dh:cb_fv * dh + fox_dim].reshape(bs, ts * hf, dh)
    c_new = g2s[:, LANE_LOGF:LANE_LOGF + hf].reshape(bs, ts * hf)
    o_fx_s = fox_sample(pt_flat, fq_s, cache_fox_k, cache_fox_v, cpg, c_new.reshape(bs, ts * hf, 1), fk_s, fv_s,
                        c_new.reshape(bs, 1, ts * hf), fnw, n_seq=bs, n_pages=n_pages, n_new=ts, pages_per_step=pg)

    cat = jnp.concatenate([jnp.concatenate([o_dn_p, o_fx_p], axis=1),
                           jnp.concatenate([o_dn_s, o_fx_s.reshape(tsm, fox_dim)], axis=1)], axis=0).astype(BF16)
    x2 = matmul(cat, w_out[lyr].astype(BF16), x, tm=640, tn=1024, name="out_proj")

    top_idx, top_w, rank, counts, packed = router(x2, ffn_norm_w[lyr], w_router[lyr], b_router[lyr], tr=320)
    pos, first_blk, num_blk, n_rows = _moe_layout(top_idx[:, :TOP_K], rank[:, :TOP_K], counts[0].astype(I32), t)
    xs = moe_dispatch(pos, packed, n_rows, tt=128)
    hact = moe_ffn(first_blk, num_blk, xs, (w_gate, w_up),
                   (b_gate[lyr].reshape(N_EXP, 1, -1), b_up[lyr].reshape(N_EXP, 1, -1)),
                   out_dtype=BF16, packed_rows=True, name="moe_up")
    y_rows = moe_ffn(first_blk, num_blk, hact, (w_down,), (b_down[lyr].reshape(N_EXP, 1, -1),),
                     out_dtype=F32, packed_rows=False, name="moe_down")
    y_p, y_s = moe_combine(pos, y_rows, x2, top_w, final_norm_w, tt=64, t_head=tp)

    npg = lp // page
    y_prompt = y_p.reshape(bp, lp, d)
    y_sample = y_s.reshape(bs, ts, d)
    fk_p = proj[:tp, cb_fk * dh:cb_fk * dh + fox_dim].reshape(1, bp, npg, page, hf, dh)
    fv_p = proj[:tp, cb_fv * dh:cb_fv * dh + fox_dim].reshape(1, bp, npg, page, hf, dh)
    lf_p = g1p[:, LANE_LOGF:LANE_LOGF + hf].reshape(1, bp, npg, page, hf)
    conv_p = proj[:tp, :conv_dim].reshape(bp, lp, conv_dim)[:, lp - (CONV_W - 1):][None]
    lf_s = g1s[:, LANE_LOGF:LANE_LOGF + hf].reshape(1, bs, ts, hf)
    conv_s = xpad.reshape(bs, frame, conv_dim)[:, frame - (CONV_W - 1):][None]
    return (y_prompt, y_sample, fk_p, fv_p, lf_p, s_fin_p[None], conv_p,
            fk_s.reshape(1, bs, ts, hf, dh), fv_s.reshape(1, bs, ts, hf, dh), lf_s, s_fin_s[None], conv_s)
```

```python
import functools
import math

import jax
import jax.numpy as jnp
from jax import lax
from jax.experimental import pallas as pl
from jax.experimental.pallas import tpu as pltpu

F32 = jnp.float32
BF16 = jnp.bfloat16
I32 = jnp.int32

HEAD_DIM = 128
N_DN_HEADS = 16
N_FOX_HEADS = 16
CONV_W = 4
DN_CHUNK = 64
N_EXP = 32
TOP_K = 4
SWIGLU_LIMIT = 7.0
SWIGLU_ALPHA = 1.702
RMS_EPS = 1e-6
L2_EPS = 1e-6

LANE_BETA = 0
LANE_G = 16
LANE_LOGF = 32

VMEM_LIMIT = 56 * 1024 * 1024
MOE_ROWS = 256
MOE_COLS = 512
MOE_WEIGHT_SLABS = 8
MOE_ROW_PARTS = 4
NEG_INF = float("-inf")


def _cparams(*sem):
    return pltpu.CompilerParams(dimension_semantics=sem, vmem_limit_bytes=VMEM_LIMIT)


def _rmsnorm_kernel(x_ref, w_ref, o_ref):
    x = x_ref[...]
    y = x * lax.rsqrt(jnp.mean(x * x, axis=-1, keepdims=True) + RMS_EPS)
    o_ref[...] = (y * w_ref[...]).astype(o_ref.dtype)


def rmsnorm_rows(x, w, out_dtype, tr):
    t, d = x.shape
    return pl.pallas_call(
        _rmsnorm_kernel,
        grid=(t // tr,),
        in_specs=[pl.BlockSpec((tr, d), lambda i: (i, 0)), pl.BlockSpec((1, d), lambda i: (0, 0))],
        out_specs=pl.BlockSpec((tr, d), lambda i: (i, 0)),
        out_shape=jax.ShapeDtypeStruct((t, d), out_dtype),
        compiler_params=_cparams("parallel"),
        name="rmsnorm",
    )(x, w.reshape(1, d))


def _mm_kernel(a_ref, b_ref, *rest, has_res):
    o_ref = rest[-1]
    acc = jnp.dot(a_ref[...], b_ref[...], preferred_element_type=F32)
    if has_res:
        acc = acc + rest[0][...]
    o_ref[...] = acc


def matmul(a, b, res=None, *, tm, tn, name):
    m, k = a.shape
    n = b.shape[1]
    in_specs = [pl.BlockSpec((tm, k), lambda j, i: (i, 0)), pl.BlockSpec((k, tn), lambda j, i: (0, j))]
    args = [a, b]
    if res is not None:
        in_specs.append(pl.BlockSpec((tm, tn), lambda j, i: (i, j)))
        args.append(res)
    return pl.pallas_call(
        functools.partial(_mm_kernel, has_res=res is not None),
        grid=(n // tn, m // tm),
        in_specs=in_specs,
        out_specs=pl.BlockSpec((tm, tn), lambda j, i: (i, j)),
        out_shape=jax.ShapeDtypeStruct((m, n), F32),
        compiler_params=_cparams("parallel", "parallel"),
        name=name,
    )(*args)


def _softplus(x):
    return jnp.maximum(x, 0.0) + jnp.log1p(jnp.exp(-jnp.abs(x)))


def _gates_kernel(s_ref, p_ref, g1_ref, g2_ref, *, seg_g, seg_f):
    x = s_ref[...]
    lane = lax.broadcasted_iota(I32, x.shape, 1)
    row = lax.broadcasted_iota(I32, x.shape, 0)
    is_b = lane < LANE_G
    is_g = jnp.logical_and(lane >= LANE_G, lane < LANE_LOGF)
    is_f = jnp.logical_and(lane >= LANE_LOGF, lane < LANE_LOGF + N_FOX_HEADS)
    beta = jax.nn.sigmoid(x)
    g = -jnp.exp(p_ref[0:1, :]) * _softplus(x + p_ref[1:2, :])
    lf = -_softplus(-(x + p_ref[2:3, :]))
    g1 = jnp.where(is_b, beta, jnp.where(is_g, g, jnp.where(is_f, lf, 0.0)))
    g1_ref[...] = g1
    c = jnp.where(is_b, 0.0, g1)
    pos = jnp.where(is_g, row & (seg_g - 1), row & (seg_f - 1))
    s = 1
    while s < max(seg_g, seg_f):
        c = c + jnp.where(pos >= s, pltpu.roll(c, s, axis=0), 0.0)
        s *= 2
    g2_ref[...] = c


def gates(small, params, *, rows_blk, seg_g, seg_f):
    t = small.shape[0]
    spec = pl.BlockSpec((rows_blk, 128), lambda i: (i, 0))
    return pl.pallas_call(
        functools.partial(_gates_kernel, seg_g=seg_g, seg_f=seg_f),
        grid=(t // rows_blk,),
        in_specs=[spec, pl.BlockSpec((8, 128), lambda i: (0, 0))],
        out_specs=[spec, spec],
        out_shape=[jax.ShapeDtypeStruct((t, 128), F32)] * 2,
        compiler_params=_cparams("parallel"),
        name="gates",
    )(small, params)


def _bmm(a, b):
    return lax.dot_general(a, b, (((2,), (1,)), ((0,), (0,))), preferred_element_type=F32)


def _bmm_nt(a, b):
    return lax.dot_general(a, b, (((2,), (2,)), ((0,), (0,))), preferred_element_type=F32)


def _split_bf16(a):
    hi = a.astype(BF16)
    return hi, (a - hi.astype(F32)).astype(BF16)


def _bmm_split(a, b):
    return _bmm(a[0], b[0]) + (_bmm(a[0], b[1]) + _bmm(a[1], b[0]))


def _gdn_kernel(q_ref, k_ref, v_ref, z_ref, cwq_ref, cwk_ref, cwv_ref, g1_ref, g2_ref, gr_ref, nw_ref, *rest,
                chunk, n_chunks, pad_rows, has_s0, heads):
    if has_s0:
        s0_ref, o_ref, sf_ref, u_s, w_s, qs_s, kd_s, qk_s, el_s, o_s = rest
    else:
        o_ref, sf_ref, u_s, w_s, qs_s, kd_s, qk_s, el_s, o_s = rest
    c, n = chunk, n_chunks
    rows = c * n
    dk = HEAD_DIM
    h_first = pl.program_id(1) * heads
    row = lax.broadcasted_iota(I32, (rows, dk), 0)
    lane = lax.broadcasted_iota(I32, (rows, dk), 1)
    ii = lax.broadcasted_iota(I32, (c, c), 0)
    jj = lax.broadcasted_iota(I32, (c, c), 1)
    incl = (ii >= jj)[None]
    strict = (ii > jj)[None]
    eye = jnp.where((ii == jj)[None], 1.0, 0.0)
    scale = dk ** -0.5

    def conv_silu(x_ref, cw_ref, cols):
        x = x_ref[:, cols]
        acc = x * cw_ref[CONV_W - 1:CONV_W, cols]
        for j in range(CONV_W - 1):
            sh = CONV_W - 1 - j
            acc = acc + jnp.where(row >= sh, pltpu.roll(x, sh, axis=0), 0.0) * cw_ref[j:j + 1, cols]
        y = acc * jax.nn.sigmoid(acc)
        if pad_rows:
            y = jnp.where(row >= pad_rows, y, 0.0)
        return y

    def l2n(x):
        return x * lax.rsqrt(jnp.sum(x * x, axis=-1, keepdims=True) + L2_EPS)

    for hh in range(heads):
        cols = slice(hh * dk, (hh + 1) * dk)
        q = l2n(conv_silu(q_ref, cwq_ref, cols))
        k = l2n(conv_silu(k_ref, cwk_ref, cols))
        v = conv_silu(v_ref, cwv_ref, cols)
        beta = jnp.sum(jnp.where(lane == LANE_BETA + h_first + hh, g1_ref[...], 0.0), axis=1, keepdims=True)
        gc = jnp.sum(jnp.where(lane == LANE_G + h_first + hh, g2_ref[...], 0.0), axis=1, keepdims=True)
        q3 = q.reshape(n, c, dk)
        k3 = k.reshape(n, c, dk)
        v3 = v.reshape(n, c, dk)
        beta3 = beta.reshape(n, c, 1)
        gc3 = gc.reshape(n, c, 1)
        gr3 = gr_ref[hh]
        decay = jnp.where(incl, jnp.exp(jnp.where(incl, gc3 - gr3, 0.0)), 0.0)
        kb = k3 * beta3
        k3b = k3.astype(BF16)
        a = jnp.where(strict, _bmm_nt(kb.astype(BF16), k3b) * decay, 0.0)
        tinv = eye - a
        xs = _split_bf16(a)
        p = 2
        while p < c:
            xs = _split_bf16(_bmm_split(xs, xs))
            tinv = tinv + _bmm_split(_split_bf16(tinv), xs)
            p *= 2
        tb = tinv.astype(BF16)
        u_s[hh] = _bmm(tb, (v3 * beta3).astype(BF16))
        w_s[hh] = _bmm(tb, (kb * jnp.exp(gc3)).astype(BF16))
        qsc = q3 * scale
        qk_s[hh] = _bmm_nt(qsc.astype(BF16), k3b) * decay
        qs_s[hh] = qsc * jnp.exp(gc3)
        gl3 = gc3[:, c - 1:c, :]
        kd_s[hh] = k3 * jnp.exp(gl3 - gc3)
        el_s[hh] = jnp.broadcast_to(jnp.exp(gl3), (n, 1, dk))

    def step(i, states):
        new_states = []
        for hh in range(heads):
            s = states[hh]
            sb = s.astype(BF16)
            v_new = u_s[hh, i] - jnp.dot(w_s[hh, i].astype(BF16), sb, preferred_element_type=F32)
            vb = v_new.astype(BF16)
            o_s[hh, i] = (jnp.dot(qs_s[hh, i].astype(BF16), sb, preferred_element_type=F32)
                          + jnp.dot(qk_s[hh, i].astype(BF16), vb, preferred_element_type=F32))
            new_states.append(s * el_s[hh, i] + lax.dot_general(kd_s[hh, i].astype(BF16), vb, (((0,), (0,)), ((), ())),
                                                                preferred_element_type=F32))
        return tuple(new_states)

    init = tuple(s0_ref[hh] if has_s0 else jnp.zeros((dk, dk), F32) for hh in range(heads))
    finals = step(0, init) if n == 1 else lax.fori_loop(0, n, step, init)
    for hh in range(heads):
        cols = slice(hh * dk, (hh + 1) * dk)
        sf_ref[hh] = finals[hh]
        o = o_s[hh].reshape(rows, dk)
        on = o * lax.rsqrt(jnp.mean(o * o, axis=-1, keepdims=True) + RMS_EPS) * nw_ref[...]
        z = z_ref[:, cols]
        o_ref[:, cols] = on * (z * jax.nn.sigmoid(z))


def gdn_heads(xq, xk, xv, xz, col_off, conv_w, g1, g2, gr, norm_w, s0, *, n_seq, chunk, n_chunks, pad_rows, heads):
    rows = chunk * n_chunks
    n_h = N_DN_HEADS
    wide = heads * HEAD_DIM
    assert n_h % heads == 0 and all(off % heads == 0 for off in col_off)
    qo, ko, vo, zo = (off // heads for off in col_off)

    def xspec(off):
        return pl.BlockSpec((rows, wide), lambda b, h, off=off: (b, off + h))

    def cwspec(off):
        return pl.BlockSpec((CONV_W, wide), lambda b, h, off=off: (0, off + h))

    gspec = pl.BlockSpec((rows, 128), lambda b, h: (b, 0))
    sspec = pl.BlockSpec((None, heads, HEAD_DIM, HEAD_DIM), lambda b, h: (b, h, 0, 0))
    in_specs = [xspec(qo), xspec(ko), xspec(vo), xspec(zo),
                cwspec(0), cwspec(n_h // heads), cwspec(2 * n_h // heads), gspec, gspec,
                pl.BlockSpec((None, heads, n_chunks, 1, chunk), lambda b, h: (b, h, 0, 0, 0)),
                pl.BlockSpec((1, HEAD_DIM), lambda b, h: (0, 0))]
    args = [xq, xk, xv, xz, conv_w, conv_w, conv_w, g1, g2, gr, norm_w]
    if s0 is not None:
        in_specs.append(sspec)
        args.append(s0)
    big = pltpu.VMEM((heads, n_chunks, chunk, HEAD_DIM), F32)
    return pl.pallas_call(
        functools.partial(_gdn_kernel, chunk=chunk, n_chunks=n_chunks, pad_rows=pad_rows, has_s0=s0 is not None,
                          heads=heads),
        grid=(n_seq, n_h // heads),
        in_specs=in_specs,
        out_specs=[pl.BlockSpec((rows, wide), lambda b, h: (b, h)), sspec],
        out_shape=[jax.ShapeDtypeStruct((n_seq * rows, n_h * HEAD_DIM), F32),
                   jax.ShapeDtypeStruct((n_seq, n_h, HEAD_DIM, HEAD_DIM), F32)],
        scratch_shapes=[big, big, big, big, pltpu.VMEM((heads, n_chunks, chunk, chunk), F32),
                        pltpu.VMEM((heads, n_chunks, 1, HEAD_DIM), F32), big],
        compiler_params=_cparams("parallel", "parallel"),
        name="gdn_c%d" % chunk,
    )(*args)


def _fox_prompt_kernel(q_ref, k_ref, v_ref, g2_ref, ck_ref, nw_ref, o_ref, *, tq, tk):
    h = pl.program_id(1)
    i = pl.program_id(2)
    d = HEAD_DIM
    scale = d ** -0.5
    q = q_ref[...].astype(BF16)
    lane = lax.broadcasted_iota(I32, (tq, 128), 1)
    cq = jnp.sum(jnp.where(lane == LANE_LOGF + h, g2_ref[...], 0.0), axis=1, keepdims=True)
    causal = lax.broadcasted_iota(I32, (tq, tk), 0) >= lax.broadcasted_iota(I32, (tq, tk), 1)

    def block(j, carry, on_diagonal):
        m, l, acc = carry
        start = pl.multiple_of(j * tk, tk)
        kj = k_ref[pl.ds(start, tk), :].astype(BF16)
        vj = v_ref[pl.ds(start, tk), :].astype(BF16)
        s = lax.dot_general(q, kj, (((1,), (1,)), ((), ())), preferred_element_type=F32) * scale
        s = s + cq - ck_ref[j]
        if on_diagonal:
            s = jnp.where(causal, s, NEG_INF)
        m_new = jnp.maximum(m, jnp.max(s, axis=1, keepdims=True))
        p = jnp.exp(s - m_new)
        corr = jnp.exp(m - m_new)
        l = l * corr + jnp.sum(p, axis=1, keepdims=True)
        acc = acc * corr + jnp.dot(p.astype(BF16), vj, preferred_element_type=F32)
        return m_new, l, acc

    init = (jnp.full((tq, 1), NEG_INF, F32), jnp.zeros((tq, 1), F32), jnp.zeros((tq, d), F32))
    carry = lax.fori_loop(0, i, lambda j, c: block(j, c, False), init)
    _, l, acc = block(i, carry, True)
    o = acc / l
    o_ref[...] = o * lax.rsqrt(jnp.mean(o * o, axis=-1, keepdims=True) + RMS_EPS) * nw_ref[...]


def fox_prompt(proj, col_off, g2, ck, norm_w, *, n_seq, seq, tq, tk):
    assert tq == tk
    qo, ko, vo = col_off
    hh = N_FOX_HEADS
    nq = seq // tq
    return pl.pallas_call(
        functools.partial(_fox_prompt_kernel, tq=tq, tk=tk),
        grid=(n_seq, hh, nq),
        in_specs=[pl.BlockSpec((tq, HEAD_DIM), lambda b, h, i: (b * nq + i, qo + h)),
                  pl.BlockSpec((seq, HEAD_DIM), lambda b, h, i: (b, ko + h)),
                  pl.BlockSpec((seq, HEAD_DIM), lambda b, h, i: (b, vo + h)),
                  pl.BlockSpec((tq, 128), lambda b, h, i: (b * nq + i, 0)),
                  pl.BlockSpec((None, None, seq // tk, 1, tk), lambda b, h, i: (b, h, 0, 0, 0)),
                  pl.BlockSpec((1, HEAD_DIM), lambda b, h, i: (0, 0))],
        out_specs=pl.BlockSpec((tq, HEAD_DIM), lambda b, h, i: (b * nq + i, h)),
        out_shape=jax.ShapeDtypeStruct((n_seq * seq, hh * HEAD_DIM), F32),
        compiler_params=_cparams("parallel", "parallel", "arbitrary"),
        name="fox_prompt",
    )(proj, proj, proj, g2, ck, norm_w)


def _fox_cpast_kernel(pt_ref, *rest, pages_per_step):
    del pt_ref
    lf_refs = rest[:pages_per_step]
    o_ref, carry = rest[pages_per_step:]
    p = pl.program_id(1)

    @pl.when(p == 0)
    def _():
        carry[...] = jnp.zeros_like(carry)

    ps = lf_refs[0].shape[0]
    ii = lax.broadcasted_iota(I32, (ps, ps), 0)
    jj = lax.broadcasted_iota(I32, (ps, ps), 1)
    later = jnp.where(jj > ii, 1.0, 0.0)
    run = carry[...]
    for i in range(pages_per_step):
        lf = lf_refs[i][...]
        excl = jnp.dot(later, lf, precision=lax.Precision.HIGHEST, preferred_element_type=F32)
        o_ref[pages_per_step - 1 - i] = -(excl + run)
        run = run + jnp.sum(lf, axis=0, keepdims=True)
    carry[...] = run


def fox_cpast(page_table_flat, cache_logf, *, n_seq, n_pages, pages_per_step):
    _, _, ps, hh = cache_logf.shape
    g = pages_per_step
    n_steps = n_pages // g

    def page_spec(i):
        return pl.BlockSpec((None, None, ps, hh),
                            lambda b, p, pt, i=i: (0, pt[b * n_pages + n_pages - 1 - p * g - i], 0, 0))

    grid_spec = pltpu.PrefetchScalarGridSpec(
        num_scalar_prefetch=1,
        grid=(n_seq, n_steps),
        in_specs=[page_spec(i) for i in range(g)],
        out_specs=pl.BlockSpec((None, g, ps, hh), lambda b, p, pt: (b, n_steps - 1 - p, 0, 0)),
        scratch_shapes=[pltpu.VMEM((1, hh), F32)],
    )
    return pl.pallas_call(
        functools.partial(_fox_cpast_kernel, pages_per_step=g),
        grid_spec=grid_spec,
        out_shape=jax.ShapeDtypeStruct((n_seq, n_pages, ps, hh), F32),
        compiler_params=_cparams("parallel", "arbitrary"),
        name="fox_cpast",
    )(page_table_flat, *([cache_logf] * g))


def _fox_sample_kernel(pt_ref, q_ref, *rest, n_steps, n_new, pages_per_step):
    del pt_ref
    g = pages_per_step
    k_refs = rest[:g]
    v_refs = rest[g:2 * g]
    cpg_ref, cq_ref, kn_ref, vn_ref, cn_ref, nw_ref, o_ref, m_s, l_s, acc_s = rest[2 * g:]
    p = pl.program_id(1)
    d = HEAD_DIM
    hh = N_FOX_HEADS
    nr = n_new * hh

    @pl.when(p == 0)
    def _():
        m_s[...] = jnp.full_like(m_s, NEG_INF)
        l_s[...] = jnp.zeros_like(l_s)
        acc_s[...] = jnp.zeros_like(acc_s)

    q = (q_ref[...] * (d ** -0.5)).astype(BF16)
    cq = cq_ref[...]

    def update(scores, vals):
        m = m_s[...]
        m_new = m
        for s in scores:
            m_new = jnp.maximum(m_new, jnp.max(s, axis=1, keepdims=True))
        corr = jnp.exp(m - m_new)
        l = l_s[...] * corr
        acc = acc_s[...] * corr
        for s, val in zip(scores, vals):
            pr = jnp.exp(s - m_new)
            l = l + jnp.sum(pr, axis=1, keepdims=True)
            acc = acc + jnp.dot(pr.astype(BF16), val, preferred_element_type=F32)
        l_s[...] = l
        acc_s[...] = acc
        m_s[...] = m_new

    ps = k_refs[0].shape[0]
    nc = ps * hh
    same_head = ((lax.broadcasted_iota(I32, (nr, nc), 0) & (hh - 1))
                 == (lax.broadcasted_iota(I32, (nr, nc), 1) & (hh - 1)))
    scores, vals = [], []
    for i in range(g):
        kf = k_refs[i][...].reshape(nc, d).astype(BF16)
        s = lax.dot_general(q, kf, (((1,), (1,)), ((), ())), preferred_element_type=F32)
        s = s + cq - cpg_ref[:, i * nc:(i + 1) * nc]
        scores.append(jnp.where(same_head, s, NEG_INF))
        vals.append(v_refs[i][...].reshape(nc, d).astype(BF16))
    update(scores, vals)

    @pl.when(p == n_steps - 1)
    def _():
        kn = kn_ref[...].astype(BF16)
        vn = vn_ref[...].astype(BF16)
        s2 = lax.dot_general(q, kn, (((1,), (1,)), ((), ())), preferred_element_type=F32)
        s2 = s2 + cq - cn_ref[...]
        r = lax.broadcasted_iota(I32, (nr, nr), 0)
        c = lax.broadcasted_iota(I32, (nr, nr), 1)
        keep = jnp.logical_and((r & (hh - 1)) == (c & (hh - 1)), r >= c)
        update([jnp.where(keep, s2, NEG_INF)], [vn])
        o = acc_s[...] / l_s[...]
        o_ref[...] = o * lax.rsqrt(jnp.mean(o * o, axis=-1, keepdims=True) + RMS_EPS) * nw_ref[...]


def fox_sample(page_table_flat, q, cache_k, cache_v, cpg, cq, k_new, v_new, c_new, norm_w, *, n_seq, n_pages, n_new,
               pages_per_step):
    _, _, ps, hh, d = cache_k.shape
    nr = n_new * hh
    g = pages_per_step
    n_steps = n_pages // g

    def cache_spec(i):
        return pl.BlockSpec((None, None, ps, hh, d), lambda b, p, pt, i=i: (0, pt[b * n_pages + p * g + i], 0, 0, 0))

    seq_spec = pl.BlockSpec((None, nr, d), lambda b, p, pt: (b, 0, 0))
    grid_spec = pltpu.PrefetchScalarGridSpec(
        num_scalar_prefetch=1,
        grid=(n_seq, n_steps),
        in_specs=[seq_spec] + [cache_spec(i) for i in range(g)] * 2
        + [pl.BlockSpec((None, 1, g * ps * hh), lambda b, p, pt: (b * n_steps + p, 0, 0)),
           pl.BlockSpec((None, nr, 1), lambda b, p, pt: (b, 0, 0)),
           seq_spec, seq_spec,
           pl.BlockSpec((None, 1, nr), lambda b, p, pt: (b, 0, 0)),
           pl.BlockSpec((1, d), lambda b, p, pt: (0, 0))],
        out_specs=seq_spec,
        scratch_shapes=[pltpu.VMEM((nr, 1), F32), pltpu.VMEM((nr, 1), F32), pltpu.VMEM((nr, d), F32)],
    )
    return pl.pallas_call(
        functools.partial(_fox_sample_kernel, n_steps=n_steps, n_new=n_new, pages_per_step=g),
        grid_spec=grid_spec,
        out_shape=jax.ShapeDtypeStruct((n_seq, nr, d), F32),
        compiler_params=_cparams("parallel", "arbitrary"),
        name="fox_sample",
    )(page_table_flat, q, *([cache_k] * g), *([cache_v] * g), cpg, cq, k_new, v_new, c_new, norm_w)


def _router_kernel(x_ref, nw_ref, wr_ref, br_ref, idx_ref, w_ref, rank_ref, cnt_ref, hp_ref, run):
    i = pl.program_id(0)

    @pl.when(i == 0)
    def _():
        run[...] = jnp.zeros_like(run)

    x = x_ref[...]
    hcur = x * lax.rsqrt(jnp.mean(x * x, axis=-1, keepdims=True) + RMS_EPS) * nw_ref[...]
    logits = jnp.dot(hcur, wr_ref[...], precision=lax.Precision.HIGHEST, preferred_element_type=F32) + br_ref[...]
    tr, ne = logits.shape
    lane = lax.broadcasted_iota(I32, (tr, ne), 1)
    lane_f = lane.astype(F32)
    olane = lax.broadcasted_iota(I32, (tr, 128), 1)
    idx_out = jnp.zeros((tr, 128), I32)
    w_out = jnp.zeros((tr, 128), F32)
    cur = logits
    tops, hots = [], []
    for kk in range(TOP_K):
        mx = jnp.max(cur, axis=1, keepdims=True)
        sel = jnp.min(jnp.where(cur == mx, lane_f, float(ne)), axis=1, keepdims=True).astype(I32)
        tops.append(mx)
        hots.append(lane == sel)
        idx_out = jnp.where(olane == kk, sel, idx_out)
        cur = jnp.where(lane == sel, NEG_INF, cur)
    es = [jnp.exp(t - tops[0]) for t in tops]
    tot = es[0]
    for e in es[1:]:
        tot = tot + e
    for kk in range(TOP_K):
        w_out = jnp.where(olane == kk, es[kk] / tot, w_out)
    idx_ref[...] = idx_out
    w_ref[...] = w_out

    chosen = jnp.zeros((tr, ne), F32)
    for hot in hots:
        chosen = jnp.where(hot, 1.0, chosen)
    before = jnp.where(lax.broadcasted_iota(I32, (tr, tr), 0) > lax.broadcasted_iota(I32, (tr, tr), 1), 1.0, 0.0)
    ahead = jnp.dot(before.astype(BF16), chosen.astype(BF16), preferred_element_type=F32) + run[...]
    rank_out = jnp.zeros((tr, 128), I32)
    for kk in range(TOP_K):
        r = jnp.sum(jnp.where(hots[kk], ahead, 0.0), axis=1, keepdims=True).astype(I32)
        rank_out = jnp.where(olane == kk, r, rank_out)
    rank_ref[...] = rank_out
    run[...] = run[...] + jnp.sum(chosen, axis=0, keepdims=True)
    cnt_ref[...] = run[...]

    half = hcur.shape[1] // 2
    lo = pltpu.bitcast(hcur[:, :half].astype(BF16).astype(F32), jnp.uint32)
    hi = pltpu.bitcast(hcur[:, half:].astype(BF16).astype(F32), jnp.uint32)
    hp_ref[...] = (hi & jnp.uint32(0xFFFF0000)) | (lo >> jnp.uint32(16))


def _unpack_bf16_pairs(words):
    lo = pltpu.bitcast(words << jnp.uint32(16), F32).astype(BF16)
    hi = pltpu.bitcast(words & jnp.uint32(0xFFFF0000), F32).astype(BF16)
    return jnp.concatenate([lo, hi], axis=1)


def router(x, norm_w, w_router, b_router, *, tr):
    t, d = x.shape
    ne = w_router.shape[1]
    tile = pl.BlockSpec((tr, 128), lambda i: (i, 0))
    return pl.pallas_call(
        _router_kernel,
        grid=(t // tr,),
        in_specs=[pl.BlockSpec((tr, d), lambda i: (i, 0)), pl.BlockSpec((1, d), lambda i: (0, 0)),
                  pl.BlockSpec((d, ne), lambda i: (0, 0)), pl.BlockSpec((1, ne), lambda i: (0, 0))],
        out_specs=[tile, tile, tile, pl.BlockSpec((1, ne), lambda i: (0, 0)),
                   pl.BlockSpec((tr, d // 2), lambda i: (i, 0))],
        out_shape=[jax.ShapeDtypeStruct((t, 128), I32), jax.ShapeDtypeStruct((t, 128), F32),
                   jax.ShapeDtypeStruct((t, 128), I32), jax.ShapeDtypeStruct((1, ne), F32),
                   jax.ShapeDtypeStruct((t, d // 2), jnp.uint32)],
        scratch_shapes=[pltpu.VMEM((1, ne), F32)],
        compiler_params=_cparams("arbitrary"),
        name="router",
    )(x, norm_w.reshape(1, d), w_router, b_router.reshape(1, ne))


def _row_copy(src_hbm, dst, sem, src_row, dst_row):
    return pltpu.make_async_copy(src_hbm.at[pl.ds(src_row, 1), :], dst.at[pl.ds(dst_row, 1), :], sem)


def _moe_gather_kernel(tok_ref, src_hbm, o_ref, buf, sem, *, tg):
    i = pl.program_id(0)
    n = pl.num_programs(0)

    def start_block(blk, slot):
        def issue(r, carry):
            _row_copy(src_hbm, buf.at[slot], sem.at[slot], tok_ref[blk * tg + r], r).start()
            return carry
        lax.fori_loop(0, tg, issue, 0)

    @pl.when(i == 0)
    def _():
        start_block(0, 0)

    @pl.when(i + 1 < n)
    def _():
        start_block(i + 1, (i + 1) % 2)

    slot = i % 2

    def wait(r, carry):
        _row_copy(src_hbm, buf.at[slot], sem.at[slot], 0, r).wait()
        return carry
    lax.fori_loop(0, tg, wait, 0)
    o_ref[...] = buf[slot]


def moe_gather(row_tok, packed, *, tg):
    n_rows = row_tok.shape[0]
    dw = packed.shape[1]
    grid_spec = pltpu.PrefetchScalarGridSpec(
        num_scalar_prefetch=1,
        grid=(n_rows // tg,),
        in_specs=[pl.BlockSpec(memory_space=pl.ANY)],
        out_specs=pl.BlockSpec((tg, dw), lambda i, tok: (i, 0)),
        scratch_shapes=[pltpu.VMEM((2, tg, dw), packed.dtype), pltpu.SemaphoreType.DMA((2,))],
    )
    return pl.pallas_call(
        functools.partial(_moe_gather_kernel, tg=tg),
        grid_spec=grid_spec,
        out_shape=jax.ShapeDtypeStruct((n_rows, dw), packed.dtype),
        compiler_params=_cparams("arbitrary"),
        name="moe_gather",
    )(row_tok, packed)


class _CopyGroup:
    def __init__(self, copies):
        self.copies = copies

    def start(self):
        for cp in self.copies:
            cp.start()

    def wait(self):
        for cp in self.copies:
            cp.wait()


def _moe_ffn_kernel(fb_ref, nb_ref, x_hbm, *rest, n_w, n_j, n_split, packed_rows):
    w_refs = rest[:n_w * n_split]
    b_refs = rest[n_w * n_split:n_w * n_split + n_w]
    o_hbm, wc, xbuf, obuf, xsem, osem = rest[n_w * n_split + n_w:]
    g = pl.program_id(0)
    e = g // n_j
    j = g % n_j
    fb = fb_ref[e]
    nb = nb_ref[e]
    tm = xbuf.shape[1]
    tc = obuf.shape[2]

    def x_copy(blk, slot):
        part = tm // MOE_ROW_PARTS
        return _CopyGroup([pltpu.make_async_copy(x_hbm.at[pl.ds(blk * tm + q * part, part), :],
                                                 xbuf.at[slot, pl.ds(q * part, part), :], xsem.at[slot])
                           for q in range(MOE_ROW_PARTS)])

    def o_copy(blk, slot):
        return pltpu.make_async_copy(obuf.at[slot], o_hbm.at[pl.ds(blk * tm, tm), pl.ds(j * tc, tc)], osem.at[slot])

    @pl.when(jnp.logical_and(g == 0, nb > 0))
    def _():
        x_copy(fb, 0).start()

    slab = wc.shape[1] // n_split
    for i in range(n_w):
        for s in range(n_split):
            wc[i, s * slab:(s + 1) * slab, :] = w_refs[i * n_split + s][...].astype(BF16)

    def body(r, carry):
        slot = r % 2
        x_copy(fb + r, slot).wait()

        @pl.when(r + 1 < nb)
        def _():
            x_copy(fb + r + 1, 1 - slot).start()

        @pl.when(r >= 2)
        def _():
            o_copy(fb + r - 2, slot).wait()

        x = _unpack_bf16_pairs(xbuf[slot]) if packed_rows else xbuf[slot]
        outs = [jnp.dot(x, wc[i], preferred_element_type=F32) + b_refs[i][...] for i in range(n_w)]
        if n_w == 2:
            gt = jnp.minimum(outs[0], SWIGLU_LIMIT)
            up = jnp.clip(outs[1], -SWIGLU_LIMIT, SWIGLU_LIMIT)
            res = gt * jax.nn.sigmoid(SWIGLU_ALPHA * gt) * (up + 1.0)
        else:
            res = outs[0]
        obuf[slot] = res.astype(obuf.dtype)
        o_copy(fb + r, slot).start()
        return carry
    lax.fori_loop(0, nb, body, 0)

    @pl.when(nb >= 2)
    def _():
        o_copy(fb + nb - 2, nb % 2).wait()

    @pl.when(nb >= 1)
    def _():
        o_copy(fb + nb - 1, (nb - 1) % 2).wait()

    @pl.when(e == pl.num_programs(0) // n_j - 1)
    def _():
        n_tail = o_hbm.shape[0] // tm - (fb + nb)
        obuf[0] = jnp.zeros(obuf.shape[1:], obuf.dtype)

        def fill(r, carry):
            o_copy(fb + nb + r, 0).start()
            return carry
        lax.fori_loop(0, n_tail, fill, 0)

        def fill_done(r, carry):
            o_copy(fb + nb + r, 0).wait()
            return carry
        lax.fori_loop(0, n_tail, fill_done, 0)

    @pl.when(g + 1 < pl.num_programs(0))
    def _():
        e2 = (g + 1) // n_j

        @pl.when(nb_ref[e2] > 0)
        def _():
            x_copy(fb_ref[e2], 0).start()


def moe_ffn(first_blk, num_blk, x, weights, biases, *, out_dtype, packed_rows, name):
    n_rows, dx = x.shape
    n_exp, din, dout = weights[0].shape[1:]
    tm, tc = MOE_ROWS, MOE_COLS
    n_j = dout // tc
    n_w = len(weights)
    n_split = MOE_WEIGHT_SLABS

    def wspec(s):
        return pl.BlockSpec((None, None, din // n_split, tc), lambda g, fb, nb, s=s: (0, g // n_j, s, g % n_j))

    bspec = pl.BlockSpec((None, 1, tc), lambda g, fb, nb: (g // n_j, 0, g % n_j))
    w_args = [w for w in weights for _ in range(n_split)]
    grid_spec = pltpu.PrefetchScalarGridSpec(
        num_scalar_prefetch=2,
        grid=(n_exp * n_j,),
        in_specs=[pl.BlockSpec(memory_space=pl.ANY)] + [wspec(s) for s in range(n_split)] * n_w + [bspec] * n_w,
        out_specs=pl.BlockSpec(memory_space=pl.ANY),
        scratch_shapes=[pltpu.VMEM((n_w, din, tc), BF16), pltpu.VMEM((2, tm, dx), x.dtype),
                        pltpu.VMEM((2, tm, tc), out_dtype),
                        pltpu.SemaphoreType.DMA((2,)), pltpu.SemaphoreType.DMA((2,))],
    )
    return pl.pallas_call(
        functools.partial(_moe_ffn_kernel, n_w=n_w, n_j=n_j, n_split=n_split, packed_rows=packed_rows),
        grid_spec=grid_spec,
        out_shape=jax.ShapeDtypeStruct((n_rows, dout), out_dtype),
        compiler_params=_cparams("arbitrary"),
        name=name,
    )(first_blk, num_blk, x, *w_args, *biases)


def _moe_combine_kernel(pos_ref, y_hbm, x_ref, w_ref, nw_ref, o_head_ref, o_tail_ref, buf, sem, *, tt, n_head):
    i = pl.program_id(0)
    n = pl.num_programs(0)

    def start_block(blk, slot):
        def issue(r, carry):
            for kk in range(TOP_K):
                _row_copy(y_hbm, buf.at[slot, kk], sem.at[slot], pos_ref[(blk * tt + r) * TOP_K + kk], r).start()
            return carry
        lax.fori_loop(0, tt, issue, 0)

    @pl.when(i == 0)
    def _():
        start_block(0, 0)

    @pl.when(i + 1 < n)
    def _():
        start_block(i + 1, (i + 1) % 2)

    slot = i % 2

    def wait(r, carry):
        for kk in range(TOP_K):
            _row_copy(y_hbm, buf.at[slot, kk], sem.at[slot], 0, r).wait()
        return carry
    lax.fori_loop(0, tt, wait, 0)
    acc = x_ref[...]
    w = w_ref[...]
    for kk in range(TOP_K):
        acc = acc + w[:, kk:kk + 1] * buf[slot, kk]
    y = acc * lax.rsqrt(jnp.mean(acc * acc, axis=-1, keepdims=True) + RMS_EPS) * nw_ref[...]

    @pl.when(i < n_head)
    def _():
        o_head_ref[...] = y

    @pl.when(i >= n_head)
    def _():
        o_tail_ref[...] = y


def moe_combine(pos, y_rows, x, top_w, norm_w, *, tt, t_head):
    t, d = x.shape
    n_head = t_head // tt
    grid_spec = pltpu.PrefetchScalarGridSpec(
        num_scalar_prefetch=1,
        grid=(t // tt,),
        in_specs=[pl.BlockSpec(memory_space=pl.ANY),
                  pl.BlockSpec((tt, d), lambda i, pos: (i, 0)),
                  pl.BlockSpec((tt, 128), lambda i, pos: (i, 0)),
                  pl.BlockSpec((1, d), lambda i, pos: (0, 0))],
        out_specs=[pl.BlockSpec((tt, d), lambda i, pos: (jnp.minimum(i, n_head - 1), 0)),
                   pl.BlockSpec((tt, d), lambda i, pos: (jnp.maximum(i - n_head, 0), 0))],
        scratch_shapes=[pltpu.VMEM((2, TOP_K, tt, d), F32), pltpu.SemaphoreType.DMA((2,))],
    )
    return pl.pallas_call(
        functools.partial(_moe_combine_kernel, tt=tt, n_head=n_head),
        grid_spec=grid_spec,
        out_shape=[jax.ShapeDtypeStruct((t_head, d), F32), jax.ShapeDtypeStruct((t - t_head, d), F32)],
        compiler_params=_cparams("arbitrary"),
        name="moe_combine",
    )(pos, y_rows, x, top_w, norm_w.reshape(1, d))


def _moe_layout(top_idx, rank, counts, n_tok):
    tm = MOE_ROWS
    n_blocks = -(-(n_tok * TOP_K) // tm) + N_EXP
    padded = (counts + tm - 1) // tm * tm
    pstart = jnp.cumsum(padded) - padded
    pos = (pstart[top_idx] + rank).reshape(-1).astype(I32)
    return pos, (pstart // tm).astype(I32), (padded // tm).astype(I32), n_blocks * tm


def kernel(x_prompt, x_sample, cache_fox_k, cache_fox_v, cache_fox_logf, state_dn, state_dn_conv, page_table,
           attn_norm_w, w_in, dn_conv_w, dn_a_log, dn_dt_bias, fox_f_bias, dn_norm_w, fox_norm_w, w_out,
           ffn_norm_w, w_router, b_router, w_gate, b_gate, w_up, b_up, w_down, b_down, final_norm_w):
    bp, lp, d = x_prompt.shape
    bs, ts, _ = x_sample.shape
    tp, tsm = bp * lp, bs * ts
    t = tp + tsm
    hd, hf, dh = N_DN_HEADS, N_FOX_HEADS, HEAD_DIM
    qk_dim = hd * dh
    conv_dim = 3 * qk_dim
    fox_dim = hf * dh
    n_pages = page_table.shape[1]
    page = cache_fox_k.shape[2]
    lyr = 0

    x = jnp.concatenate([x_prompt.reshape(tp, d), x_sample.reshape(tsm, d)], axis=0)

    o_z = conv_dim
    o_b = o_z + qk_dim
    o_a = o_b + hd
    o_fq = o_a + hd
    o_ff = o_fq + 3 * fox_dim
    w = w_in[lyr]
    w_big = jnp.concatenate([w[:, :o_b], w[:, o_fq:o_ff]], axis=1).astype(BF16)
    w_small = jnp.concatenate([w[:, o_b:o_fq], w[:, o_ff:], jnp.zeros((d, 128 - 2 * hd - hf), F32)], axis=1).astype(BF16)
    h = rmsnorm_rows(x, attn_norm_w[lyr], BF16, tr=320)
    proj = matmul(h, w_big, tm=640, tn=1024, name="in_proj")
    small = matmul(h, w_small, tm=640, tn=128, name="in_proj_gates")
    cb_q, cb_k, cb_v, cb_z = 0, hd, 2 * hd, 3 * hd
    cb_fq = 4 * hd
    cb_fk = cb_fq + hf
    cb_fv = cb_fk + hf

    gparams = jnp.zeros((8, 128), F32)
    gparams = gparams.at[0, LANE_G:LANE_G + hd].set(dn_a_log[lyr])
    gparams = gparams.at[1, LANE_G:LANE_G + hd].set(dn_dt_bias[lyr])
    gparams = gparams.at[2, LANE_LOGF:LANE_LOGF + hf].set(fox_f_bias[lyr])
    chunk_p = math.gcd(lp, DN_CHUNK)
    g1p, g2p = gates(small[:tp], gparams, rows_blk=lp, seg_g=chunk_p, seg_f=lp)
    g1s, g2s = gates(small[tp:], gparams, rows_blk=tsm, seg_g=ts, seg_f=ts)

    nc_p = lp // chunk_p
    gr_p = g2p[:, LANE_G:LANE_G + hd].reshape(bp, nc_p, chunk_p, hd).transpose(0, 3, 1, 2).reshape(bp, hd, nc_p, 1, chunk_p)
    conv_w = dn_conv_w[lyr]
    dnw = dn_norm_w[lyr].reshape(1, dh)
    o_dn_p, s_fin_p = gdn_heads(proj, proj, proj, proj, (cb_q, cb_k, cb_v, cb_z), conv_w, g1p, g2p, gr_p, dnw, None,
                                n_seq=bp, chunk=chunk_p, n_chunks=nc_p, pad_rows=0, heads=2)

    frame = 8
    lead = frame - ts
    proj_s = proj[tp:].reshape(bs, ts, -1)
    hist = jnp.concatenate([jnp.zeros((bs, lead - (CONV_W - 1), conv_dim), F32), state_dn_conv[lyr]], axis=1)
    xpad = jnp.concatenate([hist, proj_s[:, :, :conv_dim]], axis=1).reshape(bs * frame, conv_dim)
    z_s = jnp.pad(proj_s[:, :, conv_dim:conv_dim + qk_dim], ((0, 0), (lead, 0), (0, 0))).reshape(bs * frame, qk_dim)
    g1s8 = jnp.pad(g1s.reshape(bs, ts, 128), ((0, 0), (lead, 0), (0, 0))).reshape(bs * frame, 128)
    g2s8 = jnp.pad(g2s.reshape(bs, ts, 128), ((0, 0), (lead, 0), (0, 0))).reshape(bs * frame, 128)
    gr_s = g2s8[:, LANE_G:LANE_G + hd].reshape(bs, frame, hd).transpose(0, 2, 1).reshape(bs, hd, 1, 1, frame)
    o_dn_s8, s_fin_s = gdn_heads(xpad, xpad, xpad, z_s, (cb_q, cb_k, cb_v, 0), conv_w, g1s8, g2s8, gr_s, dnw,
                                 state_dn[lyr], n_seq=bs, chunk=frame, n_chunks=1, pad_rows=lead, heads=hd)
    o_dn_s = o_dn_s8.reshape(bs, frame, qk_dim)[:, lead:].reshape(tsm, qk_dim)

    tq = tk = 512
    fnw = fox_norm_w[lyr].reshape(1, dh)
    ck_p = g2p[:, LANE_LOGF:LANE_LOGF + hf].reshape(bp, lp // tk, tk, hf).transpose(0, 3, 1, 2).reshape(bp, hf, lp // tk, 1, tk)
    o_fx_p = fox_prompt(proj, (cb_fq, cb_fk, cb_fv), g2p, ck_p, fnw, n_seq=bp, seq=lp, tq=tq, tk=tk)

    pt_flat = page_table.reshape(-1).astype(I32)
    cpast = fox_cpast(pt_flat, cache_fox_logf, n_seq=bs, n_pages=n_pages, pages_per_step=16)
    pg = 4
    cpg = cpast.reshape(bs * n_pages // pg, 1, pg * page * hf)
    fq_s = proj_s[:, :, cb_fq * dh:cb_fq * dh + fox_dim].reshape(bs, ts * hf, dh)
    fk_s = proj_s[:, :, cb_fk * dh:cb_fk * dh + fox_dim].reshape(bs, ts * hf, dh)
    fv_s = proj_s[:, :, cb_fv * dh:cb_fv * dh + fox_dim].reshape(bs, ts * hf, dh)
    c_new = g2s[:, LANE_LOGF:LANE_LOGF + hf].reshape(bs, ts * hf)
    o_fx_s = fox_sample(pt_flat, fq_s, cache_fox_k, cache_fox_v, cpg, c_new.reshape(bs, ts * hf, 1), fk_s, fv_s,
                        c_new.reshape(bs, 1, ts * hf), fnw, n_seq=bs, n_pages=n_pages, n_new=ts, pages_per_step=pg)

    cat = jnp.concatenate([jnp.concatenate([o_dn_p, o_fx_p], axis=1),
                           jnp.concatenate([o_dn_s, o_fx_s.reshape(tsm, fox_dim)], axis=1)], axis=0).astype(BF16)
    x2 = matmul(cat, w_out[lyr].astype(BF16), x, tm=640, tn=1024, name="out_proj")

    top_idx, top_w, rank, counts, packed = router(x2, ffn_norm_w[lyr], w_router[lyr], b_router[lyr], tr=320)
    pos, first_blk, num_blk, n_rows = _moe_layout(top_idx[:, :TOP_K], rank[:, :TOP_K], counts[0].astype(I32), t)
    row_tok = jnp.zeros((n_rows,), I32).at[pos].set(jnp.arange(t * TOP_K, dtype=I32) // TOP_K)
    xs = moe_gather(row_tok, packed, tg=256)
    hact = moe_ffn(first_blk, num_blk, xs, (w_gate, w_up),
                   (b_gate[lyr].reshape(N_EXP, 1, -1), b_up[lyr].reshape(N_EXP, 1, -1)),
                   out_dtype=BF16, packed_rows=True, name="moe_up")
    y_rows = moe_ffn(first_blk, num_blk, hact, (w_down,), (b_down[lyr].reshape(N_EXP, 1, -1),),
                     out_dtype=F32, packed_rows=False, name="moe_down")
    y_p, y_s = moe_combine(pos, y_rows, x2, top_w, final_norm_w, tt=64, t_head=tp)

    npg = lp // page
    y_prompt = y_p.reshape(bp, lp, d)
    y_sample = y_s.reshape(bs, ts, d)
    fk_p = proj[:tp, cb_fk * dh:cb_fk * dh + fox_dim].reshape(1, bp, npg, page, hf, dh)
    fv_p = proj[:tp, cb_fv * dh:cb_fv * dh + fox_dim].reshape(1, bp, npg, page, hf, dh)
    lf_p = g1p[:, LANE_LOGF:LANE_LOGF + hf].reshape(1, bp, npg, page, hf)
    conv_p = proj[:tp, :conv_dim].reshape(bp, lp, conv_dim)[:, lp - (CONV_W - 1):][None]
    lf_s = g1s[:, LANE_LOGF:LANE_LOGF + hf].reshape(1, bs, ts, hf)
    conv_s = xpad.reshape(bs, frame, conv_dim)[:, frame - (CONV_W - 1):][None]
    return (y_prompt, y_sample, fk_p, fv_p, lf_p, s_fin_p[None], conv_p,
            fk_s.reshape(1, bs, ts, hf, dh), fv_s.reshape(1, bs, ts, hf, dh), lf_s, s_fin_s[None], conv_s)
```

```python
import functools
import math

import jax
import jax.numpy as jnp
from jax import lax
from jax.experimental import pallas as pl
from jax.experimental.pallas import tpu as pltpu

F32 = jnp.float32
BF16 = jnp.bfloat16
I32 = jnp.int32

HEAD_DIM = 128
N_DN_HEADS = 16
N_FOX_HEADS = 16
CONV_W = 4
DN_CHUNK = 64
N_EXP = 32
TOP_K = 4
SWIGLU_LIMIT = 7.0
SWIGLU_ALPHA = 1.702
RMS_EPS = 1e-6
L2_EPS = 1e-6

LANE_BETA = 0
LANE_G = 16
LANE_LOGF = 32

VMEM_LIMIT = 56 * 1024 * 1024
MOE_ROWS = 256
MOE_COLS = 1024
MOE_WEIGHT_SLABS = 8
MOE_ROW_PARTS = 4
NEG_INF = float("-inf")


def _cparams(*sem):
    return pltpu.CompilerParams(dimension_semantics=sem, vmem_limit_bytes=VMEM_LIMIT)


def _rmsnorm_kernel(x_ref, w_ref, o_ref):
    x = x_ref[...]
    y = x * lax.rsqrt(jnp.mean(x * x, axis=-1, keepdims=True) + RMS_EPS)
    o_ref[...] = (y * w_ref[...]).astype(o_ref.dtype)


def rmsnorm_rows(x, w, out_dtype, tr):
    t, d = x.shape
    return pl.pallas_call(
        _rmsnorm_kernel,
        grid=(t // tr,),
        in_specs=[pl.BlockSpec((tr, d), lambda i: (i, 0)), pl.BlockSpec((1, d), lambda i: (0, 0))],
        out_specs=pl.BlockSpec((tr, d), lambda i: (i, 0)),
        out_shape=jax.ShapeDtypeStruct((t, d), out_dtype),
        compiler_params=_cparams("parallel"),
        name="rmsnorm",
    )(x, w.reshape(1, d))


def _mm_kernel(a_ref, b_ref, *rest, has_res):
    o_ref = rest[-1]
    acc = jnp.dot(a_ref[...], b_ref[...], preferred_element_type=F32)
    if has_res:
        acc = acc + rest[0][...]
    o_ref[...] = acc


def matmul(a, b, res=None, *, tm, tn, name):
    m, k = a.shape
    n = b.shape[1]
    in_specs = [pl.BlockSpec((tm, k), lambda j, i: (i, 0)), pl.BlockSpec((k, tn), lambda j, i: (0, j))]
    args = [a, b]
    if res is not None:
        in_specs.append(pl.BlockSpec((tm, tn), lambda j, i: (i, j)))
        args.append(res)
    return pl.pallas_call(
        functools.partial(_mm_kernel, has_res=res is not None),
        grid=(n // tn, m // tm),
        in_specs=in_specs,
        out_specs=pl.BlockSpec((tm, tn), lambda j, i: (i, j)),
        out_shape=jax.ShapeDtypeStruct((m, n), F32),
        compiler_params=_cparams("parallel", "parallel"),
        name=name,
    )(*args)


def _softplus(x):
    return jnp.maximum(x, 0.0) + jnp.log1p(jnp.exp(-jnp.abs(x)))


def _gates_kernel(s_ref, p_ref, g1_ref, g2_ref, *, seg_g, seg_f):
    x = s_ref[...]
    lane = lax.broadcasted_iota(I32, x.shape, 1)
    row = lax.broadcasted_iota(I32, x.shape, 0)
    is_b = lane < LANE_G
    is_g = jnp.logical_and(lane >= LANE_G, lane < LANE_LOGF)
    is_f = jnp.logical_and(lane >= LANE_LOGF, lane < LANE_LOGF + N_FOX_HEADS)
    beta = jax.nn.sigmoid(x)
    g = -jnp.exp(p_ref[0:1, :]) * _softplus(x + p_ref[1:2, :])
    lf = -_softplus(-(x + p_ref[2:3, :]))
    g1 = jnp.where(is_b, beta, jnp.where(is_g, g, jnp.where(is_f, lf, 0.0)))
    g1_ref[...] = g1
    c = jnp.where(is_b, 0.0, g1)
    pos = jnp.where(is_g, row & (seg_g - 1), row & (seg_f - 1))
    s = 1
    while s < max(seg_g, seg_f):
        c = c + jnp.where(pos >= s, pltpu.roll(c, s, axis=0), 0.0)
        s *= 2
    g2_ref[...] = c


def gates(small, params, *, rows_blk, seg_g, seg_f):
    t = small.shape[0]
    spec = pl.BlockSpec((rows_blk, 128), lambda i: (i, 0))
    return pl.pallas_call(
        functools.partial(_gates_kernel, seg_g=seg_g, seg_f=seg_f),
        grid=(t // rows_blk,),
        in_specs=[spec, pl.BlockSpec((8, 128), lambda i: (0, 0))],
        out_specs=[spec, spec],
        out_shape=[jax.ShapeDtypeStruct((t, 128), F32)] * 2,
        compiler_params=_cparams("parallel"),
        name="gates",
    )(small, params)


def _bmm(a, b):
    return lax.dot_general(a, b, (((2,), (1,)), ((0,), (0,))), preferred_element_type=F32)


def _bmm_nt(a, b):
    return lax.dot_general(a, b, (((2,), (2,)), ((0,), (0,))), preferred_element_type=F32)


def _split_bf16(a):
    hi = a.astype(BF16)
    return hi, (a - hi.astype(F32)).astype(BF16)


def _bmm_split(a, b):
    return _bmm(a[0], b[0]) + (_bmm(a[0], b[1]) + _bmm(a[1], b[0]))


def _gdn_kernel(q_ref, k_ref, v_ref, z_ref, cwq_ref, cwk_ref, cwv_ref, g1_ref, g2_ref, gr_ref, nw_ref, *rest,
                chunk, n_chunks, pad_rows, has_s0, heads):
    if has_s0:
        s0_ref, o_ref, sf_ref, u_s, w_s, qs_s, kd_s, qk_s, el_s, o_s = rest
    else:
        o_ref, sf_ref, u_s, w_s, qs_s, kd_s, qk_s, el_s, o_s = rest
    c, n = chunk, n_chunks
    rows = c * n
    dk = HEAD_DIM
    h_first = pl.program_id(1) * heads
    row = lax.broadcasted_iota(I32, (rows, dk), 0)
    lane = lax.broadcasted_iota(I32, (rows, dk), 1)
    ii = lax.broadcasted_iota(I32, (c, c), 0)
    jj = lax.broadcasted_iota(I32, (c, c), 1)
    incl = (ii >= jj)[None]
    strict = (ii > jj)[None]
    eye = jnp.where((ii == jj)[None], 1.0, 0.0)
    scale = dk ** -0.5

    def conv_silu(x_ref, cw_ref, cols):
        x = x_ref[:, cols]
        acc = x * cw_ref[CONV_W - 1:CONV_W, cols]
        for j in range(CONV_W - 1):
            sh = CONV_W - 1 - j
            acc = acc + jnp.where(row >= sh, pltpu.roll(x, sh, axis=0), 0.0) * cw_ref[j:j + 1, cols]
        y = acc * jax.nn.sigmoid(acc)
        if pad_rows:
            y = jnp.where(row >= pad_rows, y, 0.0)
        return y

    def l2n(x):
        return x * lax.rsqrt(jnp.sum(x * x, axis=-1, keepdims=True) + L2_EPS)

    for hh in range(heads):
        cols = slice(hh * dk, (hh + 1) * dk)
        q = l2n(conv_silu(q_ref, cwq_ref, cols))
        k = l2n(conv_silu(k_ref, cwk_ref, cols))
        v = conv_silu(v_ref, cwv_ref, cols)
        beta = jnp.sum(jnp.where(lane == LANE_BETA + h_first + hh, g1_ref[...], 0.0), axis=1, keepdims=True)
        gc = jnp.sum(jnp.where(lane == LANE_G + h_first + hh, g2_ref[...], 0.0), axis=1, keepdims=True)
        q3 = q.reshape(n, c, dk)
        k3 = k.reshape(n, c, dk)
        v3 = v.reshape(n, c, dk)
        beta3 = beta.reshape(n, c, 1)
        gc3 = gc.reshape(n, c, 1)
        gr3 = gr_ref[hh]
        decay = jnp.where(incl, jnp.exp(jnp.where(incl, gc3 - gr3, 0.0)), 0.0)
        kb = k3 * beta3
        k3b = k3.astype(BF16)
        a = jnp.where(strict, _bmm_nt(kb.astype(BF16), k3b) * decay, 0.0)
        tinv = eye - a
        xs = _split_bf16(a)
        p = 2
        while p < c:
            xs = _split_bf16(_bmm_split(xs, xs))
            tinv = tinv + _bmm_split(_split_bf16(tinv), xs)
            p *= 2
        tb = tinv.astype(BF16)
        u_s[hh] = _bmm(tb, (v3 * beta3).astype(BF16))
        w_s[hh] = _bmm(tb, (kb * jnp.exp(gc3)).astype(BF16))
        qsc = q3 * scale
        qk_s[hh] = _bmm_nt(qsc.astype(BF16), k3b) * decay
        qs_s[hh] = qsc * jnp.exp(gc3)
        gl3 = gc3[:, c - 1:c, :]
        kd_s[hh] = k3 * jnp.exp(gl3 - gc3)
        el_s[hh] = jnp.broadcast_to(jnp.exp(gl3), (n, 1, dk))

    def step(i, states):
        new_states = []
        for hh in range(heads):
            s = states[hh]
            sb = s.astype(BF16)
            v_new = u_s[hh, i] - jnp.dot(w_s[hh, i].astype(BF16), sb, preferred_element_type=F32)
            vb = v_new.astype(BF16)
            o_s[hh, i] = (jnp.dot(qs_s[hh, i].astype(BF16), sb, preferred_element_type=F32)
                          + jnp.dot(qk_s[hh, i].astype(BF16), vb, preferred_element_type=F32))
            new_states.append(s * el_s[hh, i] + lax.dot_general(kd_s[hh, i].astype(BF16), vb, (((0,), (0,)), ((), ())),
                                                                preferred_element_type=F32))
        return tuple(new_states)

    init = tuple(s0_ref[hh] if has_s0 else jnp.zeros((dk, dk), F32) for hh in range(heads))
    finals = step(0, init) if n == 1 else lax.fori_loop(0, n, step, init)
    for hh in range(heads):
        cols = slice(hh * dk, (hh + 1) * dk)
        sf_ref[hh] = finals[hh]
        o = o_s[hh].reshape(rows, dk)
        on = o * lax.rsqrt(jnp.mean(o * o, axis=-1, keepdims=True) + RMS_EPS) * nw_ref[...]
        z = z_ref[:, cols]
        o_ref[:, cols] = on * (z * jax.nn.sigmoid(z))


def gdn_heads(xq, xk, xv, xz, col_off, conv_w, g1, g2, gr, norm_w, s0, *, n_seq, chunk, n_chunks, pad_rows, heads):
    rows = chunk * n_chunks
    n_h = N_DN_HEADS
    wide = heads * HEAD_DIM
    assert n_h % heads == 0 and all(off % heads == 0 for off in col_off)
    qo, ko, vo, zo = (off // heads for off in col_off)

    def xspec(off):
        return pl.BlockSpec((rows, wide), lambda b, h, off=off: (b, off + h))

    def cwspec(off):
        return pl.BlockSpec((CONV_W, wide), lambda b, h, off=off: (0, off + h))

    gspec = pl.BlockSpec((rows, 128), lambda b, h: (b, 0))
    sspec = pl.BlockSpec((None, heads, HEAD_DIM, HEAD_DIM), lambda b, h: (b, h, 0, 0))
    in_specs = [xspec(qo), xspec(ko), xspec(vo), xspec(zo),
                cwspec(0), cwspec(n_h // heads), cwspec(2 * n_h // heads), gspec, gspec,
                pl.BlockSpec((None, heads, n_chunks, 1, chunk), lambda b, h: (b, h, 0, 0, 0)),
                pl.BlockSpec((1, HEAD_DIM), lambda b, h: (0, 0))]
    args = [xq, xk, xv, xz, conv_w, conv_w, conv_w, g1, g2, gr, norm_w]
    if s0 is not None:
        in_specs.append(sspec)
        args.append(s0)
    big = pltpu.VMEM((heads, n_chunks, chunk, HEAD_DIM), F32)
    return pl.pallas_call(
        functools.partial(_gdn_kernel, chunk=chunk, n_chunks=n_chunks, pad_rows=pad_rows, has_s0=s0 is not None,
                          heads=heads),
        grid=(n_seq, n_h // heads),
        in_specs=in_specs,
        out_specs=[pl.BlockSpec((rows, wide), lambda b, h: (b, h)), sspec],
        out_shape=[jax.ShapeDtypeStruct((n_seq * rows, n_h * HEAD_DIM), F32),
                   jax.ShapeDtypeStruct((n_seq, n_h, HEAD_DIM, HEAD_DIM), F32)],
        scratch_shapes=[big, big, big, big, pltpu.VMEM((heads, n_chunks, chunk, chunk), F32),
                        pltpu.VMEM((heads, n_chunks, 1, HEAD_DIM), F32), big],
        compiler_params=_cparams("parallel", "parallel"),
        name="gdn_c%d" % chunk,
    )(*args)


def _fox_prompt_kernel(q_ref, k_ref, v_ref, g2_ref, ck_ref, nw_ref, o_ref, *, tq, tk):
    h = pl.program_id(1)
    i = pl.program_id(2)
    d = HEAD_DIM
    scale = d ** -0.5
    q = q_ref[...].astype(BF16)
    lane = lax.broadcasted_iota(I32, (tq, 128), 1)
    cq = jnp.sum(jnp.where(lane == LANE_LOGF + h, g2_ref[...], 0.0), axis=1, keepdims=True)
    causal = lax.broadcasted_iota(I32, (tq, tk), 0) >= lax.broadcasted_iota(I32, (tq, tk), 1)

    def block(j, carry, on_diagonal):
        m, l, acc = carry
        start = pl.multiple_of(j * tk, tk)
        kj = k_ref[pl.ds(start, tk), :].astype(BF16)
        vj = v_ref[pl.ds(start, tk), :].astype(BF16)
        s = lax.dot_general(q, kj, (((1,), (1,)), ((), ())), preferred_element_type=F32) * scale
        s = s + cq - ck_ref[j]
        if on_diagonal:
            s = jnp.where(causal, s, NEG_INF)
        m_new = jnp.maximum(m, jnp.max(s, axis=1, keepdims=True))
        p = jnp.exp(s - m_new)
        corr = jnp.exp(m - m_new)
        l = l * corr + jnp.sum(p, axis=1, keepdims=True)
        acc = acc * corr + jnp.dot(p.astype(BF16), vj, preferred_element_type=F32)
        return m_new, l, acc

    init = (jnp.full((tq, 1), NEG_INF, F32), jnp.zeros((tq, 1), F32), jnp.zeros((tq, d), F32))
    carry = lax.fori_loop(0, i, lambda j, c: block(j, c, False), init)
    _, l, acc = block(i, carry, True)
    o = acc / l
    o_ref[...] = o * lax.rsqrt(jnp.mean(o * o, axis=-1, keepdims=True) + RMS_EPS) * nw_ref[...]


def fox_prompt(proj, col_off, g2, ck, norm_w, *, n_seq, seq, tq, tk):
    assert tq == tk
    qo, ko, vo = col_off
    hh = N_FOX_HEADS
    nq = seq // tq
    return pl.pallas_call(
        functools.partial(_fox_prompt_kernel, tq=tq, tk=tk),
        grid=(n_seq, hh, nq),
        in_specs=[pl.BlockSpec((tq, HEAD_DIM), lambda b, h, i: (b * nq + i, qo + h)),
                  pl.BlockSpec((seq, HEAD_DIM), lambda b, h, i: (b, ko + h)),
                  pl.BlockSpec((seq, HEAD_DIM), lambda b, h, i: (b, vo + h)),
                  pl.BlockSpec((tq, 128), lambda b, h, i: (b * nq + i, 0)),
                  pl.BlockSpec((None, None, seq // tk, 1, tk), lambda b, h, i: (b, h, 0, 0, 0)),
                  pl.BlockSpec((1, HEAD_DIM), lambda b, h, i: (0, 0))],
        out_specs=pl.BlockSpec((tq, HEAD_DIM), lambda b, h, i: (b * nq + i, h)),
        out_shape=jax.ShapeDtypeStruct((n_seq * seq, hh * HEAD_DIM), F32),
        compiler_params=_cparams("parallel", "parallel", "arbitrary"),
        name="fox_prompt",
    )(proj, proj, proj, g2, ck, norm_w)


def _fox_cpast_kernel(pt_ref, *rest, pages_per_step):
    del pt_ref
    lf_refs = rest[:pages_per_step]
    o_ref, carry = rest[pages_per_step:]
    p = pl.program_id(1)

    @pl.when(p == 0)
    def _():
        carry[...] = jnp.zeros_like(carry)

    ps = lf_refs[0].shape[0]
    ii = lax.broadcasted_iota(I32, (ps, ps), 0)
    jj = lax.broadcasted_iota(I32, (ps, ps), 1)
    later = jnp.where(jj > ii, 1.0, 0.0)
    run = carry[...]
    for i in range(pages_per_step):
        lf = lf_refs[i][...]
        excl = jnp.dot(later, lf, precision=lax.Precision.HIGHEST, preferred_element_type=F32)
        o_ref[pages_per_step - 1 - i] = -(excl + run)
        run = run + jnp.sum(lf, axis=0, keepdims=True)
    carry[...] = run


def fox_cpast(page_table_flat, cache_logf, *, n_seq, n_pages, pages_per_step):
    _, _, ps, hh = cache_logf.shape
    g = pages_per_step
    n_steps = n_pages // g

    def page_spec(i):
        return pl.BlockSpec((None, None, ps, hh),
                            lambda b, p, pt, i=i: (0, pt[b * n_pages + n_pages - 1 - p * g - i], 0, 0))

    grid_spec = pltpu.PrefetchScalarGridSpec(
        num_scalar_prefetch=1,
        grid=(n_seq, n_steps),
        in_specs=[page_spec(i) for i in range(g)],
        out_specs=pl.BlockSpec((None, g, ps, hh), lambda b, p, pt: (b, n_steps - 1 - p, 0, 0)),
        scratch_shapes=[pltpu.VMEM((1, hh), F32)],
    )
    return pl.pallas_call(
        functools.partial(_fox_cpast_kernel, pages_per_step=g),
        grid_spec=grid_spec,
        out_shape=jax.ShapeDtypeStruct((n_seq, n_pages, ps, hh), F32),
        compiler_params=_cparams("parallel", "arbitrary"),
        name="fox_cpast",
    )(page_table_flat, *([cache_logf] * g))


def _fox_sample_kernel(pt_ref, q_ref, *rest, n_steps, n_new, pages_per_step):
    del pt_ref
    g = pages_per_step
    k_refs = rest[:g]
    v_refs = rest[g:2 * g]
    cpg_ref, cq_ref, kn_ref, vn_ref, cn_ref, nw_ref, o_ref, m_s, l_s, acc_s = rest[2 * g:]
    p = pl.program_id(1)
    d = HEAD_DIM
    hh = N_FOX_HEADS
    nr = n_new * hh

    @pl.when(p == 0)
    def _():
        m_s[...] = jnp.full_like(m_s, NEG_INF)
        l_s[...] = jnp.zeros_like(l_s)
        acc_s[...] = jnp.zeros_like(acc_s)

    q = (q_ref[...] * (d ** -0.5)).astype(BF16)
    cq = cq_ref[...]

    def update(scores, vals):
        m = m_s[...]
        m_new = m
        for s in scores:
            m_new = jnp.maximum(m_new, jnp.max(s, axis=1, keepdims=True))
        corr = jnp.exp(m - m_new)
        l = l_s[...] * corr
        acc = acc_s[...] * corr
        for s, val in zip(scores, vals):
            pr = jnp.exp(s - m_new)
            l = l + jnp.sum(pr, axis=1, keepdims=True)
            acc = acc + jnp.dot(pr.astype(BF16), val, preferred_element_type=F32)
        l_s[...] = l
        acc_s[...] = acc
        m_s[...] = m_new

    ps = k_refs[0].shape[0]
    nc = ps * hh
    same_head = ((lax.broadcasted_iota(I32, (nr, nc), 0) & (hh - 1))
                 == (lax.broadcasted_iota(I32, (nr, nc), 1) & (hh - 1)))
    scores, vals = [], []
    for i in range(g):
        kf = k_refs[i][...].reshape(nc, d).astype(BF16)
        s = lax.dot_general(q, kf, (((1,), (1,)), ((), ())), preferred_element_type=F32)
        s = s + cq - cpg_ref[:, i * nc:(i + 1) * nc]
        scores.append(jnp.where(same_head, s, NEG_INF))
        vals.append(v_refs[i][...].reshape(nc, d).astype(BF16))
    update(scores, vals)

    @pl.when(p == n_steps - 1)
    def _():
        kn = kn_ref[...].astype(BF16)
        vn = vn_ref[...].astype(BF16)
        s2 = lax.dot_general(q, kn, (((1,), (1,)), ((), ())), preferred_element_type=F32)
        s2 = s2 + cq - cn_ref[...]
        r = lax.broadcasted_iota(I32, (nr, nr), 0)
        c = lax.broadcasted_iota(I32, (nr, nr), 1)
        keep = jnp.logical_and((r & (hh - 1)) == (c & (hh - 1)), r >= c)
        update([jnp.where(keep, s2, NEG_INF)], [vn])
        o = acc_s[...] / l_s[...]
        o_ref[...] = o * lax.rsqrt(jnp.mean(o * o, axis=-1, keepdims=True) + RMS_EPS) * nw_ref[...]


def fox_sample(page_table_flat, q, cache_k, cache_v, cpg, cq, k_new, v_new, c_new, norm_w, *, n_seq, n_pages, n_new,
               pages_per_step):
    _, _, ps, hh, d = cache_k.shape
    nr = n_new * hh
    g = pages_per_step
    n_steps = n_pages // g

    def cache_spec(i):
        return pl.BlockSpec((None, None, ps, hh, d), lambda b, p, pt, i=i: (0, pt[b * n_pages + p * g + i], 0, 0, 0))

    seq_spec = pl.BlockSpec((None, nr, d), lambda b, p, pt: (b, 0, 0))
    grid_spec = pltpu.PrefetchScalarGridSpec(
        num_scalar_prefetch=1,
        grid=(n_seq, n_steps),
        in_specs=[seq_spec] + [cache_spec(i) for i in range(g)] * 2
        + [pl.BlockSpec((None, 1, g * ps * hh), lambda b, p, pt: (b * n_steps + p, 0, 0)),
           pl.BlockSpec((None, nr, 1), lambda b, p, pt: (b, 0, 0)),
           seq_spec, seq_spec,
           pl.BlockSpec((None, 1, nr), lambda b, p, pt: (b, 0, 0)),
           pl.BlockSpec((1, d), lambda b, p, pt: (0, 0))],
        out_specs=seq_spec,
        scratch_shapes=[pltpu.VMEM((nr, 1), F32), pltpu.VMEM((nr, 1), F32), pltpu.VMEM((nr, d), F32)],
    )
    return pl.pallas_call(
        functools.partial(_fox_sample_kernel, n_steps=n_steps, n_new=n_new, pages_per_step=g),
        grid_spec=grid_spec,
        out_shape=jax.ShapeDtypeStruct((n_seq, nr, d), F32),
        compiler_params=_cparams("parallel", "arbitrary"),
        name="fox_sample",
    )(page_table_flat, q, *([cache_k] * g), *([cache_v] * g), cpg, cq, k_new, v_new, c_new, norm_w)


def _router_kernel(x_ref, nw_ref, wr_ref, br_ref, idx_ref, w_ref, rank_ref, cnt_ref, hp_ref, run):
    i = pl.program_id(0)

    @pl.when(i == 0)
    def _():
        run[...] = jnp.zeros_like(run)

    x = x_ref[...]
    hcur = x * lax.rsqrt(jnp.mean(x * x, axis=-1, keepdims=True) + RMS_EPS) * nw_ref[...]
    logits = jnp.dot(hcur, wr_ref[...], precision=lax.Precision.HIGHEST, preferred_element_type=F32) + br_ref[...]
    tr, ne = logits.shape
    lane = lax.broadcasted_iota(I32, (tr, ne), 1)
    lane_f = lane.astype(F32)
    olane = lax.broadcasted_iota(I32, (tr, 128), 1)
    idx_out = jnp.zeros((tr, 128), I32)
    w_out = jnp.zeros((tr, 128), F32)
    cur = logits
    tops, hots = [], []
    for kk in range(TOP_K):
        mx = jnp.max(cur, axis=1, keepdims=True)
        sel = jnp.min(jnp.where(cur == mx, lane_f, float(ne)), axis=1, keepdims=True).astype(I32)
        tops.append(mx)
        hots.append(lane == sel)
        idx_out = jnp.where(olane == kk, sel, idx_out)
        cur = jnp.where(lane == sel, NEG_INF, cur)
    es = [jnp.exp(t - tops[0]) for t in tops]
    tot = es[0]
    for e in es[1:]:
        tot = tot + e
    for kk in range(TOP_K):
        w_out = jnp.where(olane == kk, es[kk] / tot, w_out)
    idx_ref[...] = idx_out
    w_ref[...] = w_out

    chosen = jnp.zeros((tr, ne), F32)
    for hot in hots:
        chosen = jnp.where(hot, 1.0, chosen)
    before = jnp.where(lax.broadcasted_iota(I32, (tr, tr), 0) > lax.broadcasted_iota(I32, (tr, tr), 1), 1.0, 0.0)
    ahead = jnp.dot(before.astype(BF16), chosen.astype(BF16), preferred_element_type=F32) + run[...]
    rank_out = jnp.zeros((tr, 128), I32)
    for kk in range(TOP_K):
        r = jnp.sum(jnp.where(hots[kk], ahead, 0.0), axis=1, keepdims=True).astype(I32)
        rank_out = jnp.where(olane == kk, r, rank_out)
    rank_ref[...] = rank_out
    run[...] = run[...] + jnp.sum(chosen, axis=0, keepdims=True)
    cnt_ref[...] = run[...]

    half = hcur.shape[1] // 2
    lo = pltpu.bitcast(hcur[:, :half].astype(BF16).astype(F32), jnp.uint32)
    hi = pltpu.bitcast(hcur[:, half:].astype(BF16).astype(F32), jnp.uint32)
    hp_ref[...] = (hi & jnp.uint32(0xFFFF0000)) | (lo >> jnp.uint32(16))


def _unpack_bf16_pairs(words):
    lo = pltpu.bitcast(words << jnp.uint32(16), F32).astype(BF16)
    hi = pltpu.bitcast(words & jnp.uint32(0xFFFF0000), F32).astype(BF16)
    return jnp.concatenate([lo, hi], axis=1)


def router(x, norm_w, w_router, b_router, *, tr):
    t, d = x.shape
    ne = w_router.shape[1]
    tile = pl.BlockSpec((tr, 128), lambda i: (i, 0))
    return pl.pallas_call(
        _router_kernel,
        grid=(t // tr,),
        in_specs=[pl.BlockSpec((tr, d), lambda i: (i, 0)), pl.BlockSpec((1, d), lambda i: (0, 0)),
                  pl.BlockSpec((d, ne), lambda i: (0, 0)), pl.BlockSpec((1, ne), lambda i: (0, 0))],
        out_specs=[tile, tile, tile, pl.BlockSpec((1, ne), lambda i: (0, 0)),
                   pl.BlockSpec((tr, d // 2), lambda i: (i, 0))],
        out_shape=[jax.ShapeDtypeStruct((t, 128), I32), jax.ShapeDtypeStruct((t, 128), F32),
                   jax.ShapeDtypeStruct((t, 128), I32), jax.ShapeDtypeStruct((1, ne), F32),
                   jax.ShapeDtypeStruct((t, d // 2), jnp.uint32)],
        scratch_shapes=[pltpu.VMEM((1, ne), F32)],
        compiler_params=_cparams("arbitrary"),
        name="router",
    )(x, norm_w.reshape(1, d), w_router, b_router.reshape(1, ne))


def _row_copy(src_hbm, dst, sem, src_row, dst_row):
    return pltpu.make_async_copy(src_hbm.at[pl.ds(src_row, 1), :], dst.at[pl.ds(dst_row, 1), :], sem)


def _moe_gather_kernel(tok_ref, src_hbm, o_ref, buf, sem, *, tg):
    i = pl.program_id(0)
    n = pl.num_programs(0)

    def start_block(blk, slot):
        def issue(r, carry):
            _row_copy(src_hbm, buf.at[slot], sem.at[slot], tok_ref[blk * tg + r], r).start()
            return carry
        lax.fori_loop(0, tg, issue, 0)

    @pl.when(i == 0)
    def _():
        start_block(0, 0)

    @pl.when(i + 1 < n)
    def _():
        start_block(i + 1, (i + 1) % 2)

    slot = i % 2

    def wait(r, carry):
        _row_copy(src_hbm, buf.at[slot], sem.at[slot], 0, r).wait()
        return carry
    lax.fori_loop(0, tg, wait, 0)
    o_ref[...] = buf[slot]


def moe_gather(row_tok, packed, *, tg):
    n_rows = row_tok.shape[0]
    dw = packed.shape[1]
    grid_spec = pltpu.PrefetchScalarGridSpec(
        num_scalar_prefetch=1,
        grid=(n_rows // tg,),
        in_specs=[pl.BlockSpec(memory_space=pl.ANY)],
        out_specs=pl.BlockSpec((tg, dw), lambda i, tok: (i, 0)),
        scratch_shapes=[pltpu.VMEM((2, tg, dw), packed.dtype), pltpu.SemaphoreType.DMA((2,))],
    )
    return pl.pallas_call(
        functools.partial(_moe_gather_kernel, tg=tg),
        grid_spec=grid_spec,
        out_shape=jax.ShapeDtypeStruct((n_rows, dw), packed.dtype),
        compiler_params=_cparams("arbitrary"),
        name="moe_gather",
    )(row_tok, packed)


class _CopyGroup:
    def __init__(self, copies):
        self.copies = copies

    def start(self):
        for cp in self.copies:
            cp.start()

    def wait(self):
        for cp in self.copies:
            cp.wait()


def _moe_ffn_kernel(fb_ref, nb_ref, x_hbm, *rest, n_w, n_j, n_split, packed_rows):
    w_refs = rest[:n_w * n_split]
    b_refs = rest[n_w * n_split:n_w * n_split + n_w]
    o_hbm, wc, xbuf, obuf, xsem, osem = rest[n_w * n_split + n_w:]
    g = pl.program_id(0)
    e = g // n_j
    j = g % n_j
    fb = fb_ref[e]
    nb = nb_ref[e]
    tm = xbuf.shape[1]
    tc = obuf.shape[2]

    def x_copy(blk, slot):
        part = tm // MOE_ROW_PARTS
        return _CopyGroup([pltpu.make_async_copy(x_hbm.at[pl.ds(blk * tm + q * part, part), :],
                                                 xbuf.at[slot, pl.ds(q * part, part), :], xsem.at[slot])
                           for q in range(MOE_ROW_PARTS)])

    def o_copy(blk, slot):
        return pltpu.make_async_copy(obuf.at[slot], o_hbm.at[pl.ds(blk * tm, tm), pl.ds(j * tc, tc)], osem.at[slot])

    @pl.when(jnp.logical_and(g == 0, nb > 0))
    def _():
        x_copy(fb, 0).start()

    slab = wc.shape[1] // n_split
    for i in range(n_w):
        for s in range(n_split):
            wc[i, s * slab:(s + 1) * slab, :] = w_refs[i * n_split + s][...].astype(BF16)

    def body(r, carry):
        slot = r % 2
        x_copy(fb + r, slot).wait()

        @pl.when(r + 1 < nb)
        def _():
            x_copy(fb + r + 1, 1 - slot).start()

        @pl.when(r >= 2)
        def _():
            o_copy(fb + r - 2, slot).wait()

        x = _unpack_bf16_pairs(xbuf[slot]) if packed_rows else xbuf[slot]
        outs = [jnp.dot(x, wc[i], preferred_element_type=F32) + b_refs[i][...] for i in range(n_w)]
        if n_w == 2:
            gt = jnp.minimum(outs[0], SWIGLU_LIMIT)
            up = jnp.clip(outs[1], -SWIGLU_LIMIT, SWIGLU_LIMIT)
            res = gt * jax.nn.sigmoid(SWIGLU_ALPHA * gt) * (up + 1.0)
        else:
            res = outs[0]
        obuf[slot] = res.astype(obuf.dtype)
        o_copy(fb + r, slot).start()
        return carry
    lax.fori_loop(0, nb, body, 0)

    @pl.when(nb >= 2)
    def _():
        o_copy(fb + nb - 2, nb % 2).wait()

    @pl.when(nb >= 1)
    def _():
        o_copy(fb + nb - 1, (nb - 1) % 2).wait()

    @pl.when(e == pl.num_programs(0) // n_j - 1)
    def _():
        n_tail = o_hbm.shape[0] // tm - (fb + nb)
        obuf[0] = jnp.zeros(obuf.shape[1:], obuf.dtype)

        def fill(r, carry):
            o_copy(fb + nb + r, 0).start()
            return carry
        lax.fori_loop(0, n_tail, fill, 0)

        def fill_done(r, carry):
            o_copy(fb + nb + r, 0).wait()
            return carry
        lax.fori_loop(0, n_tail, fill_done, 0)

    @pl.when(g + 1 < pl.num_programs(0))
    def _():
        e2 = (g + 1) // n_j

        @pl.when(nb_ref[e2] > 0)
        def _():
            x_copy(fb_ref[e2], 0).start()


def moe_ffn(first_blk, num_blk, x, weights, biases, *, out_dtype, packed_rows, name):
    n_rows, dx = x.shape
    n_exp, din, dout = weights[0].shape[1:]
    n_w = len(weights)
    tm, tc = MOE_ROWS, MOE_COLS // n_w
    n_j = dout // tc
    n_split = MOE_WEIGHT_SLABS

    def wspec(s):
        return pl.BlockSpec((None, None, din // n_split, tc), lambda g, fb, nb, s=s: (0, g // n_j, s, g % n_j))

    bspec = pl.BlockSpec((None, 1, tc), lambda g, fb, nb: (g // n_j, 0, g % n_j))
    w_args = [w for w in weights for _ in range(n_split)]
    grid_spec = pltpu.PrefetchScalarGridSpec(
        num_scalar_prefetch=2,
        grid=(n_exp * n_j,),
        in_specs=[pl.BlockSpec(memory_space=pl.ANY)] + [wspec(s) for s in range(n_split)] * n_w + [bspec] * n_w,
        out_specs=pl.BlockSpec(memory_space=pl.ANY),
        scratch_shapes=[pltpu.VMEM((n_w, din, tc), BF16), pltpu.VMEM((2, tm, dx), x.dtype),
                        pltpu.VMEM((2, tm, tc), out_dtype),
                        pltpu.SemaphoreType.DMA((2,)), pltpu.SemaphoreType.DMA((2,))],
    )
    return pl.pallas_call(
        functools.partial(_moe_ffn_kernel, n_w=n_w, n_j=n_j, n_split=n_split, packed_rows=packed_rows),
        grid_spec=grid_spec,
        out_shape=jax.ShapeDtypeStruct((n_rows, dout), out_dtype),
        compiler_params=_cparams("arbitrary"),
        name=name,
    )(first_blk, num_blk, x, *w_args, *biases)


def _moe_combine_kernel(pos_ref, y_hbm, x_ref, w_ref, nw_ref, o_head_ref, o_tail_ref, buf, sem, *, tt, n_head):
    i = pl.program_id(0)
    n = pl.num_programs(0)

    def start_block(blk, slot):
        def issue(r, carry):
            for kk in range(TOP_K):
                _row_copy(y_hbm, buf.at[slot, kk], sem.at[slot], pos_ref[(blk * tt + r) * TOP_K + kk], r).start()
            return carry
        lax.fori_loop(0, tt, issue, 0)

    @pl.when(i == 0)
    def _():
        start_block(0, 0)

    @pl.when(i + 1 < n)
    def _():
        start_block(i + 1, (i + 1) % 2)

    slot = i % 2

    def wait(r, carry):
        for kk in range(TOP_K):
            _row_copy(y_hbm, buf.at[slot, kk], sem.at[slot], 0, r).wait()
        return carry
    lax.fori_loop(0, tt, wait, 0)
    acc = x_ref[...]
    w = w_ref[...]
    for kk in range(TOP_K):
        acc = acc + w[:, kk:kk + 1] * buf[slot, kk]
    y = acc * lax.rsqrt(jnp.mean(acc * acc, axis=-1, keepdims=True) + RMS_EPS) * nw_ref[...]

    @pl.when(i < n_head)
    def _():
        o_head_ref[...] = y

    @pl.when(i >= n_head)
    def _():
        o_tail_ref[...] = y


def moe_combine(pos, y_rows, x, top_w, norm_w, *, tt, t_head):
    t, d = x.shape
    n_head = t_head // tt
    grid_spec = pltpu.PrefetchScalarGridSpec(
        num_scalar_prefetch=1,
        grid=(t // tt,),
        in_specs=[pl.BlockSpec(memory_space=pl.ANY),
                  pl.BlockSpec((tt, d), lambda i, pos: (i, 0)),
                  pl.BlockSpec((tt, 128), lambda i, pos: (i, 0)),
                  pl.BlockSpec((1, d), lambda i, pos: (0, 0))],
        out_specs=[pl.BlockSpec((tt, d), lambda i, pos: (jnp.minimum(i, n_head - 1), 0)),
                   pl.BlockSpec((tt, d), lambda i, pos: (jnp.maximum(i - n_head, 0), 0))],
        scratch_shapes=[pltpu.VMEM((2, TOP_K, tt, d), F32), pltpu.SemaphoreType.DMA((2,))],
    )
    return pl.pallas_call(
        functools.partial(_moe_combine_kernel, tt=tt, n_head=n_head),
        grid_spec=grid_spec,
        out_shape=[jax.ShapeDtypeStruct((t_head, d), F32), jax.ShapeDtypeStruct((t - t_head, d), F32)],
        compiler_params=_cparams("arbitrary"),
        name="moe_combine",
    )(pos, y_rows, x, top_w, norm_w.reshape(1, d))


def _moe_layout(top_idx, rank, counts, n_tok):
    tm = MOE_ROWS
    n_blocks = -(-(n_tok * TOP_K) // tm) + N_EXP
    padded = (counts + tm - 1) // tm * tm
    pstart = jnp.cumsum(padded) - padded
    pos = (pstart[top_idx] + rank).reshape(-1).astype(I32)
    return pos, (pstart // tm).astype(I32), (padded // tm).astype(I32), n_blocks * tm


def kernel(x_prompt, x_sample, cache_fox_k, cache_fox_v, cache_fox_logf, state_dn, state_dn_conv, page_table,
           attn_norm_w, w_in, dn_conv_w, dn_a_log, dn_dt_bias, fox_f_bias, dn_norm_w, fox_norm_w, w_out,
           ffn_norm_w, w_router, b_router, w_gate, b_gate, w_up, b_up, w_down, b_down, final_norm_w):
    bp, lp, d = x_prompt.shape
    bs, ts, _ = x_sample.shape
    tp, tsm = bp * lp, bs * ts
    t = tp + tsm
    hd, hf, dh = N_DN_HEADS, N_FOX_HEADS, HEAD_DIM
    qk_dim = hd * dh
    conv_dim = 3 * qk_dim
    fox_dim = hf * dh
    n_pages = page_table.shape[1]
    page = cache_fox_k.shape[2]
    lyr = 0

    x = jnp.concatenate([x_prompt.reshape(tp, d), x_sample.reshape(tsm, d)], axis=0)

    o_z = conv_dim
    o_b = o_z + qk_dim
    o_a = o_b + hd
    o_fq = o_a + hd
    o_ff = o_fq + 3 * fox_dim
    w = w_in[lyr]
    w_big = jnp.concatenate([w[:, :o_b], w[:, o_fq:o_ff]], axis=1).astype(BF16)
    w_small = jnp.concatenate([w[:, o_b:o_fq], w[:, o_ff:], jnp.zeros((d, 128 - 2 * hd - hf), F32)], axis=1).astype(BF16)
    h = rmsnorm_rows(x, attn_norm_w[lyr], BF16, tr=320)
    proj = matmul(h, w_big, tm=640, tn=1024, name="in_proj")
    small = matmul(h, w_small, tm=640, tn=128, name="in_proj_gates")
    cb_q, cb_k, cb_v, cb_z = 0, hd, 2 * hd, 3 * hd
    cb_fq = 4 * hd
    cb_fk = cb_fq + hf
    cb_fv = cb_fk + hf

    gparams = jnp.zeros((8, 128), F32)
    gparams = gparams.at[0, LANE_G:LANE_G + hd].set(dn_a_log[lyr])
    gparams = gparams.at[1, LANE_G:LANE_G + hd].set(dn_dt_bias[lyr])
    gparams = gparams.at[2, LANE_LOGF:LANE_LOGF + hf].set(fox_f_bias[lyr])
    chunk_p = math.gcd(lp, DN_CHUNK)
    g1p, g2p = gates(small[:tp], gparams, rows_blk=lp, seg_g=chunk_p, seg_f=lp)
    g1s, g2s = gates(small[tp:], gparams, rows_blk=tsm, seg_g=ts, seg_f=ts)

    nc_p = lp // chunk_p
    gr_p = g2p[:, LANE_G:LANE_G + hd].reshape(bp, nc_p, chunk_p, hd).transpose(0, 3, 1, 2).reshape(bp, hd, nc_p, 1, chunk_p)
    conv_w = dn_conv_w[lyr]
    dnw = dn_norm_w[lyr].reshape(1, dh)
    o_dn_p, s_fin_p = gdn_heads(proj, proj, proj, proj, (cb_q, cb_k, cb_v, cb_z), conv_w, g1p, g2p, gr_p, dnw, None,
                                n_seq=bp, chunk=chunk_p, n_chunks=nc_p, pad_rows=0, heads=2)

    frame = 8
    lead = frame - ts
    proj_s = proj[tp:].reshape(bs, ts, -1)
    hist = jnp.concatenate([jnp.zeros((bs, lead - (CONV_W - 1), conv_dim), F32), state_dn_conv[lyr]], axis=1)
    xpad = jnp.concatenate([hist, proj_s[:, :, :conv_dim]], axis=1).reshape(bs * frame, conv_dim)
    z_s = jnp.pad(proj_s[:, :, conv_dim:conv_dim + qk_dim], ((0, 0), (lead, 0), (0, 0))).reshape(bs * frame, qk_dim)
    g1s8 = jnp.pad(g1s.reshape(bs, ts, 128), ((0, 0), (lead, 0), (0, 0))).reshape(bs * frame, 128)
    g2s8 = jnp.pad(g2s.reshape(bs, ts, 128), ((0, 0), (lead, 0), (0, 0))).reshape(bs * frame, 128)
    gr_s = g2s8[:, LANE_G:LANE_G + hd].reshape(bs, frame, hd).transpose(0, 2, 1).reshape(bs, hd, 1, 1, frame)
    o_dn_s8, s_fin_s = gdn_heads(xpad, xpad, xpad, z_s, (cb_q, cb_k, cb_v, 0), conv_w, g1s8, g2s8, gr_s, dnw,
                                 state_dn[lyr], n_seq=bs, chunk=frame, n_chunks=1, pad_rows=lead, heads=hd)
    o_dn_s = o_dn_s8.reshape(bs, frame, qk_dim)[:, lead:].reshape(tsm, qk_dim)

    tq = tk = 512
    fnw = fox_norm_w[lyr].reshape(1, dh)
    ck_p = g2p[:, LANE_LOGF:LANE_LOGF + hf].reshape(bp, lp // tk, tk, hf).transpose(0, 3, 1, 2).reshape(bp, hf, lp // tk, 1, tk)
    o_fx_p = fox_prompt(proj, (cb_fq, cb_fk, cb_fv), g2p, ck_p, fnw, n_seq=bp, seq=lp, tq=tq, tk=tk)

    pt_flat = page_table.reshape(-1).astype(I32)
    cpast = fox_cpast(pt_flat, cache_fox_logf, n_seq=bs, n_pages=n_pages, pages_per_step=16)
    pg = 4
    cpg = cpast.reshape(bs * n_pages // pg, 1, pg * page * hf)
    fq_s = proj_s[:, :, cb_fq * dh:cb_fq * dh + fox_dim].reshape(bs, ts * hf, dh)
    fk_s = proj_s[:, :, cb_fk * dh:cb_fk * dh + fox_dim].reshape(bs, ts * hf, dh)
    fv_s = proj_s[:, :, cb_fv * dh:cb_fv * dh + fox_dim].reshape(bs, ts * hf, dh)
    c_new = g2s[:, LANE_LOGF:LANE_LOGF + hf].reshape(bs, ts * hf)
    o_fx_s = fox_sample(pt_flat, fq_s, cache_fox_k, cache_fox_v, cpg, c_new.reshape(bs, ts * hf, 1), fk_s, fv_s,
                        c_new.reshape(bs, 1, ts * hf), fnw, n_seq=bs, n_pages=n_pages, n_new=ts, pages_per_step=pg)

    cat = jnp.concatenate([jnp.concatenate([o_dn_p, o_fx_p], axis=1),
                           jnp.concatenate([o_dn_s, o_fx_s.reshape(tsm, fox_dim)], axis=1)], axis=0).astype(BF16)
    x2 = matmul(cat, w_out[lyr].astype(BF16), x, tm=640, tn=1024, name="out_proj")

    top_idx, top_w, rank, counts, packed = router(x2, ffn_norm_w[lyr], w_router[lyr], b_router[lyr], tr=320)
    pos, first_blk, num_blk, n_rows = _moe_layout(top_idx[:, :TOP_K], rank[:, :TOP_K], counts[0].astype(I32), t)
    row_tok = jnp.zeros((n_rows,), I32).at[pos].set(jnp.arange(t * TOP_K, dtype=I32) // TOP_K)
    xs = moe_gather(row_tok, packed, tg=256)
    hact = moe_ffn(first_blk, num_blk, xs, (w_gate, w_up),
                   (b_gate[lyr].reshape(N_EXP, 1, -1), b_up[lyr].reshape(N_EXP, 1, -1)),
                   out_dtype=BF16, packed_rows=True, name="moe_up")
    y_rows = moe_ffn(first_blk, num_blk, hact, (w_down,), (b_down[lyr].reshape(N_EXP, 1, -1),),
                     out_dtype=F32, packed_rows=False, name="moe_down")
    y_p, y_s = moe_combine(pos, y_rows, x2, top_w, final_norm_w, tt=64, t_head=tp)

    npg = lp // page
    y_prompt = y_p.reshape(bp, lp, d)
    y_sample = y_s.reshape(bs, ts, d)
    fk_p = proj[:tp, cb_fk * dh:cb_fk * dh + fox_dim].reshape(1, bp, npg, page, hf, dh)
    fv_p = proj[:tp, cb_fv * dh:cb_fv * dh + fox_dim].reshape(1, bp, npg, page, hf, dh)
    lf_p = g1p[:, LANE_LOGF:LANE_LOGF + hf].reshape(1, bp, npg, page, hf)
    conv_p = proj[:tp, :conv_dim].reshape(bp, lp, conv_dim)[:, lp - (CONV_W - 1):][None]
    lf_s = g1s[:, LANE_LOGF:LANE_LOGF + hf].reshape(1, bs, ts, hf)
    conv_s = xpad.reshape(bs, frame, conv_dim)[:, frame - (CONV_W - 1):][None]
    return (y_prompt, y_sample, fk_p, fv_p, lf_p, s_fin_p[None], conv_p,
            fk_s.reshape(1, bs, ts, hf, dh), fv_s.reshape(1, bs, ts, hf, dh), lf_s, s_fin_s[None], conv_s)
```
